```python
import jax, jax.numpy as jnp
from jax import lax
import numpy as np

D_MODEL = 1024
BATCH = 8
SEQ = 4096
DEPTH = 4

N_MIXERS = 2
N_GLA_LAYERS = (DEPTH + 1) // 2
N_SWA_LAYERS = DEPTH // 2

GLA_HEADS = 4
GLA_DK = D_MODEL // 2
GLA_DV = D_MODEL
GLA_HK = GLA_DK // GLA_HEADS
GLA_HV = GLA_DV // GLA_HEADS
GLA_GATE_RANK = 16
GLA_GATE_NORMALIZER = 16.0
GLA_CHUNK = 64
GLA_IN = 2 * GLA_DK + 2 * GLA_DV + GLA_GATE_RANK

SWA_HEAD_DIM = 64
SWA_Q_HEADS = D_MODEL // SWA_HEAD_DIM
SWA_GROUP = 8
SWA_KV_HEADS = SWA_Q_HEADS // SWA_GROUP
SWA_WINDOW = 128
SWA_BLOCK = SWA_WINDOW
SWA_IN = (SWA_Q_HEADS + 2 * SWA_KV_HEADS) * SWA_HEAD_DIM

D_FF = 4 * D_MODEL
NORM_EPS = 1e-6

kernel_name = "hybrid_gla_swa_sink_sqrelu_sandwich"


def rms_norm(x, w):
    xf = x.astype(jnp.float32)
    y = xf * lax.rsqrt(jnp.mean(xf * xf, axis=-1, keepdims=True) + NORM_EPS)
    return (y * w.astype(jnp.float32)).astype(x.dtype)


def gla_mixer(h, w_in, w_gate_up, b_gate_up, g_norm, w_out):
    B, T, _ = h.shape
    nc = T // GLA_CHUNK
    proj = h @ w_in
    q, k, v, g, glr = jnp.split(
        proj, [GLA_DK, 2 * GLA_DK, 2 * GLA_DK + GLA_DV, 2 * GLA_DK + 2 * GLA_DV], axis=-1)
    log_a = jax.nn.log_sigmoid((glr @ w_gate_up + b_gate_up).astype(jnp.float32)) / GLA_GATE_NORMALIZER

    def chunks(t, d):
        return t.reshape(B, nc, GLA_CHUNK, GLA_HEADS, d).astype(jnp.float32)

    qc = chunks(q, GLA_HK) * (GLA_HK ** -0.5)
    kc = chunks(k, GLA_HK)
    vc = chunks(v, GLA_HV)
    bc = jnp.cumsum(chunks(log_a, GLA_HK), axis=2)
    b_last = bc[:, :, -1:]
    q_dec = qc * jnp.exp(bc)
    k_inv = kc * jnp.exp(-bc)
    k_end = kc * jnp.exp(b_last - bc)

    causal = jnp.tril(jnp.ones((GLA_CHUNK, GLA_CHUNK), dtype=bool))
    att = jnp.einsum('bncha,bnsha->bnhcs', q_dec, k_inv)
    att = jnp.where(causal, att, 0.0)
    o_intra = jnp.einsum('bnhcs,bnshv->bnchv', att, vc)

    dS = jnp.einsum('bncha,bnchv->bnhav', k_end, vc)
    decay = jnp.exp(b_last[:, :, 0])

    def step(S, inp):
        dS_n, dec_n = inp
        return S * dec_n[..., None] + dS_n, S

    S0 = jnp.zeros((B, GLA_HEADS, GLA_HK, GLA_HV), jnp.float32)
    _, S_prev = lax.scan(step, S0, (jnp.moveaxis(dS, 1, 0), jnp.moveaxis(decay, 1, 0)))
    S_prev = jnp.moveaxis(S_prev, 0, 1)
    o_inter = jnp.einsum('bncha,bnhav->bnchv', q_dec, S_prev)

    o = (o_intra + o_inter).reshape(B, T, GLA_HEADS, GLA_HV)
    o = rms_norm(o, g_norm).reshape(B, T, GLA_DV)
    o = o * jax.nn.silu(g.astype(jnp.float32))
    return o.astype(h.dtype) @ w_out


def swa_sink_mixer(h, w_in, b_in, sinks, w_out, b_out):
    B, T, _ = h.shape
    nb = T // SWA_BLOCK
    proj = h @ w_in + b_in
    q, k, v = jnp.split(
        proj, [SWA_Q_HEADS * SWA_HEAD_DIM, (SWA_Q_HEADS + SWA_KV_HEADS) * SWA_HEAD_DIM], axis=-1)
    q = q.reshape(B, nb, SWA_BLOCK, SWA_KV_HEADS, SWA_GROUP, SWA_HEAD_DIM)

    def banded(t):
        t = t.reshape(B, nb, SWA_BLOCK, SWA_KV_HEADS, SWA_HEAD_DIM)
        prev = jnp.pad(t[:, :-1], ((0, 0), (1, 0), (0, 0), (0, 0), (0, 0)))
        return jnp.concatenate([prev, t], axis=2)

    kb, vb = banded(k), banded(v)
    s = jnp.einsum('bnqkgd,bnskd->bnkgqs', q, kb).astype(jnp.float32) * (SWA_HEAD_DIM ** -0.5)
    qi = jnp.arange(SWA_BLOCK)[:, None]
    sj = jnp.arange(2 * SWA_BLOCK)[None, :]
    band = (sj > qi) & (sj <= qi + SWA_WINDOW)
    not_first = jnp.arange(nb)[:, None, None] > 0
    valid = band[None] & (not_first | (sj >= SWA_BLOCK)[None])
    s = jnp.where(valid[None, :, None, None], s, -jnp.inf)

    sink = sinks.astype(jnp.float32).reshape(SWA_KV_HEADS, SWA_GROUP)[None, None, :, :, None, None]
    mx = jnp.maximum(jnp.max(s, axis=-1, keepdims=True), sink)
    e = jnp.exp(s - mx)
    p = e / (jnp.sum(e, axis=-1, keepdims=True) + jnp.exp(sink - mx))
    o = jnp.einsum('bnkgqs,bnskd->bnqkgd', p.astype(vb.dtype), vb)
    return o.reshape(B, T, SWA_Q_HEADS * SWA_HEAD_DIM) @ w_out + b_out


def sq_relu_mlp(h, w_up, w_down):
    return jnp.square(jax.nn.relu(h @ w_up)) @ w_down


def setup_inputs(seed: int = 0) -> dict:
    key = jax.random.key(seed)
    ks = jax.random.split(key, 20)

    def dense(k, shape, fan_in):
        return jax.random.normal(k, shape, jnp.float32) * (fan_in ** -0.5)

    def gain(k, shape):
        return 1.0 + 0.05 * jax.random.normal(k, shape, jnp.float32)

    def bias(k, shape):
        return 0.02 * jax.random.normal(k, shape, jnp.float32)

    return {
        "x": jax.random.normal(ks[0], (BATCH, SEQ, D_MODEL), jnp.float32),
        "ln_mix_pre": gain(ks[1], (DEPTH, D_MODEL)),
        "ln_mix_post": gain(ks[2], (DEPTH, D_MODEL)),
        "ln_mlp_pre": gain(ks[3], (DEPTH, D_MODEL)),
        "ln_mlp_post": gain(ks[4], (DEPTH, D_MODEL)),
        "gla_w_in": dense(ks[5], (N_GLA_LAYERS, D_MODEL, GLA_IN), D_MODEL),
        "gla_w_gate_up": dense(ks[6], (N_GLA_LAYERS, GLA_GATE_RANK, GLA_DK), GLA_GATE_RANK),
        "gla_b_gate_up": bias(ks[7], (N_GLA_LAYERS, GLA_DK)),
        "gla_g_norm": gain(ks[8], (N_GLA_LAYERS, GLA_HV)),
        "gla_w_out": dense(ks[9], (N_GLA_LAYERS, GLA_DV, D_MODEL), GLA_DV),
        "swa_w_in": dense(ks[10], (N_SWA_LAYERS, D_MODEL, SWA_IN), D_MODEL),
        "swa_b_in": bias(ks[11], (N_SWA_LAYERS, SWA_IN)),
        "swa_sinks": 0.5 * jax.random.normal(ks[12], (N_SWA_LAYERS, SWA_Q_HEADS), jnp.float32),
        "swa_w_out": dense(ks[13], (N_SWA_LAYERS, SWA_Q_HEADS * SWA_HEAD_DIM, D_MODEL), SWA_Q_HEADS * SWA_HEAD_DIM),
        "swa_b_out": bias(ks[14], (N_SWA_LAYERS, D_MODEL)),
        "mlp_w_up": dense(ks[15], (DEPTH, D_MODEL, D_FF), D_MODEL),
        "mlp_w_down": dense(ks[16], (DEPTH, D_FF, D_MODEL), D_FF),
    }


def reference(x, ln_mix_pre, ln_mix_post, ln_mlp_pre, ln_mlp_post,
              gla_w_in, gla_w_gate_up, gla_b_gate_up, gla_g_norm, gla_w_out,
              swa_w_in, swa_b_in, swa_sinks, swa_w_out, swa_b_out,
              mlp_w_up, mlp_w_down):
    for i in range(DEPTH):
        j = i // N_MIXERS
        h = rms_norm(x, ln_mix_pre[i])
        if i % N_MIXERS == 0:
            m = gla_mixer(h, gla_w_in[j], gla_w_gate_up[j], gla_b_gate_up[j], gla_g_norm[j], gla_w_out[j])
        else:
            m = swa_sink_mixer(h, swa_w_in[j], swa_b_in[j], swa_sinks[j], swa_w_out[j], swa_b_out[j])
        x = x + rms_norm(m, ln_mix_post[i]).astype(x.dtype)
        h = rms_norm(x, ln_mlp_pre[i])
        f = sq_relu_mlp(h, mlp_w_up[i], mlp_w_down[i])
        x = x + rms_norm(f, ln_mlp_post[i]).astype(x.dtype)
    return x
```

```python
import functools

import jax
import jax.numpy as jnp
from jax import lax
from jax.experimental import pallas as pl
from jax.experimental.pallas import tpu as pltpu

D_MODEL = 1024
NORM_EPS = 1e-6

GLA_HEADS = 4
GLA_DK = 512
GLA_DV = 1024
GLA_HK = 128
GLA_HV = 256
GLA_RANK = 16
GLA_RANK_PAD = 128
GLA_NORMALIZER = 16.0
GLA_CHUNK = 64

SWA_HD = 64
SWA_QH = 16
SWA_GROUP = 8
SWA_KVH = 2
SWA_BLOCK = 128

D_FF = 4096

F32 = jnp.float32
BF16 = jnp.bfloat16

VMEM_LIMIT_BYTES = 56 * 1024 * 1024


def _rms(x, g):
    ms = jnp.mean(x * x, axis=-1, keepdims=True)
    return x * lax.rsqrt(ms + NORM_EPS) * g


def _const_spec(shape):
    nd = len(shape)
    return pl.BlockSpec(shape, lambda *_: (0,) * nd, pipeline_mode=pl.Buffered(1))


def _params(*sem):
    return pltpu.CompilerParams(dimension_semantics=sem, vmem_limit_bytes=VMEM_LIMIT_BYTES)


def _gla_in_kernel(x_ref, g_ref, w_ref, wlr_ref, wg_hi_ref, wg_lo_ref, bg_ref,
                   qk_ref, v_ref, gate_ref, la_ref):
    h = _rms(x_ref[...], g_ref[...]).astype(BF16)
    for j, o_ref in enumerate((qk_ref, v_ref, gate_ref)):
        o_ref[...] = jnp.dot(h, w_ref[:, j * 1024:(j + 1) * 1024],
                             preferred_element_type=F32).astype(BF16)
    glr = jnp.dot(h, wlr_ref[...], preferred_element_type=F32)
    hi = glr.astype(BF16)
    lo = (glr - hi.astype(F32)).astype(BF16)
    z = (jnp.dot(hi, wg_hi_ref[...], preferred_element_type=F32)
         + jnp.dot(lo, wg_hi_ref[...], preferred_element_type=F32)
         + jnp.dot(hi, wg_lo_ref[...], preferred_element_type=F32)
         + bg_ref[...])
    la_ref[...] = jax.nn.log_sigmoid(z) / GLA_NORMALIZER


def _gla_in(x, g, w, wlr, wg_hi, wg_lo, bg, tm):
    n = x.shape[0]
    tok = lambda c: pl.BlockSpec((tm, c), lambda i: (i, 0))
    return pl.pallas_call(
        _gla_in_kernel,
        grid=(n // tm,),
        in_specs=[tok(D_MODEL), _const_spec(g.shape), _const_spec(w.shape), _const_spec(wlr.shape),
                  _const_spec(wg_hi.shape), _const_spec(wg_lo.shape), _const_spec(bg.shape)],
        out_specs=[tok(1024), tok(1024), tok(1024), tok(GLA_DK)],
        out_shape=[jax.ShapeDtypeStruct((n, 1024), BF16)] * 3 + [jax.ShapeDtypeStruct((n, GLA_DK), F32)],
        compiler_params=_params("parallel"),
        name="gla_in",
    )(x, g, w, wlr, wg_hi, wg_lo, bg)


def _gla_core_kernel(qk_ref, v_ref, gate_ref, la_ref, gn_ref, o_ref, st_ref, *, chunks):
    @pl.when(pl.program_id(1) == 0)
    def _():
        st_ref[...] = jnp.zeros_like(st_ref)

    c = GLA_CHUNK
    row = lax.broadcasted_iota(jnp.int32, (c, c), 0)
    col = lax.broadcasted_iota(jnp.int32, (c, c), 1)
    causal = col <= row
    tril = jnp.where(causal, 1.0, 0.0).astype(BF16)
    scale = GLA_HK ** -0.5
    gn = gn_ref[...]

    for ci in range(chunks):
        rows = slice(ci * c, (ci + 1) * c)
        la = la_ref[rows, :]
        la_hi = la.astype(BF16)
        la_lo = (la - la_hi.astype(F32)).astype(BF16)
        bc = (jnp.dot(tril, la_hi, preferred_element_type=F32)
              + jnp.dot(tril, la_lo, preferred_element_type=F32))
        b_last = bc[c - 1:c, :]
        q = qk_ref[rows, :GLA_DK].astype(F32)
        k = qk_ref[rows, GLA_DK:].astype(F32)
        q_dec = (q * (scale * jnp.exp(bc))).astype(BF16)
        k_inv = (k * jnp.exp(-bc)).astype(BF16)
        k_end = (k * jnp.exp(b_last - bc)).astype(BF16)
        decay = jnp.exp(b_last)
        for h in range(GLA_HEADS):
            ks = slice(h * GLA_HK, (h + 1) * GLA_HK)
            vs = slice(h * GLA_HV, (h + 1) * GLA_HV)
            v = v_ref[rows, vs]
            att = lax.dot_general(q_dec[:, ks], k_inv[:, ks], (((1,), (1,)), ((), ())),
                                  preferred_element_type=F32)
            att = jnp.where(causal, att, 0.0).astype(BF16)
            st = st_ref[h]
            o = (jnp.dot(att, v, preferred_element_type=F32)
                 + lax.dot_general(q_dec[:, ks], st.astype(BF16), (((1,), (1,)), ((), ())),
                                   preferred_element_type=F32))
            v_t = v.astype(F32).T.astype(BF16)
            st_ref[h] = st * decay[:, ks] + jnp.dot(v_t, k_end[:, ks], preferred_element_type=F32)
            o = _rms(o, gn)
            gate = gate_ref[rows, vs].astype(F32)
            o_ref[rows, vs] = (o * (gate * jax.nn.sigmoid(gate))).astype(BF16)


def _gla_core(qk, v, gate, la, gn, batch, seq, tc):
    n = batch * seq
    steps = seq // tc
    tok = lambda c: pl.BlockSpec((tc, c), lambda b, i: (b * steps + i, 0))
    return pl.pallas_call(
        functools.partial(_gla_core_kernel, chunks=tc // GLA_CHUNK),
        grid=(batch, steps),
        in_specs=[tok(1024), tok(1024), tok(1024), tok(GLA_DK), _const_spec(gn.shape)],
        out_specs=tok(1024),
        out_shape=jax.ShapeDtypeStruct((n, GLA_DV), BF16),
        scratch_shapes=[pltpu.VMEM((GLA_HEADS, GLA_HV, GLA_HK), F32)],
        compiler_params=_params("parallel", "arbitrary"),
        name="gla_core",
    )(qk, v, gate, la, gn)


def _swa_in_kernel(x_ref, g_ref, w_ref, b_ref, q_ref, kv_ref):
    h = _rms(x_ref[...], g_ref[...]).astype(BF16)
    q_ref[...] = (jnp.dot(h, w_ref[:, :1024], preferred_element_type=F32) + b_ref[:, :1024]).astype(BF16)
    kv_ref[...] = (jnp.dot(h, w_ref[:, 1024:], preferred_element_type=F32) + b_ref[:, 1024:]).astype(BF16)


def _swa_in(x, g, w, b, tm):
    n = x.shape[0]
    tok = lambda c: pl.BlockSpec((tm, c), lambda i: (i, 0))
    return pl.pallas_call(
        _swa_in_kernel,
        grid=(n // tm,),
        in_specs=[tok(D_MODEL), _const_spec(g.shape), _const_spec(w.shape), _const_spec(b.shape)],
        out_specs=[tok(1024), tok(512)],
        out_shape=[jax.ShapeDtypeStruct((n, 1024), BF16), jax.ShapeDtypeStruct((n, 512), BF16)],
        compiler_params=_params("parallel"),
        name="swa_in",
    )(x, g, w, b)


def _swa_core_kernel(sink_ref, q_ref, kv_ref, kvp_ref, o_ref, *, blocks):
    first_tile = pl.program_id(1) == 0
    blk = SWA_BLOCK
    qi = lax.broadcasted_iota(jnp.int32, (blk, 2 * blk), 0)
    sj = lax.broadcasted_iota(jnp.int32, (blk, 2 * blk), 1)
    band = (sj > qi) & (sj <= qi + blk)
    lane = lax.broadcasted_iota(jnp.int32, (blk, 2 * SWA_HD), 1)
    low = lane < SWA_HD

    for j in range(blocks):
        rows = slice(j * blk, (j + 1) * blk)
        if j == 0:
            kv_prev = kvp_ref[...]
            valid = band & (jnp.logical_not(first_tile) | (sj >= blk))
        else:
            kv_prev = kv_ref[(j - 1) * blk:j * blk, :]
            valid = band
        kv_band = jnp.concatenate([kv_prev, kv_ref[rows, :]], axis=0)
        for kh in range(SWA_KVH):
            k2 = kv_band[:, kh * 128:(kh + 1) * 128]
            v2 = kv_band[:, 256 + kh * 128:256 + (kh + 1) * 128]
            qs = []
            for g in range(SWA_GROUP):
                pair = kh * (SWA_GROUP // 2) + g // 2
                qp = q_ref[rows, pair * 128:(pair + 1) * 128]
                keep = low if g % 2 == 0 else jnp.logical_not(low)
                qs.append(jnp.where(keep, qp, jnp.zeros_like(qp)))
            qm = jnp.concatenate(qs, axis=0)
            s = lax.dot_general(qm, k2, (((1,), (1,)), ((), ())),
                                preferred_element_type=F32)
            ps = []
            for g in range(SWA_GROUP):
                sink = sink_ref[kh * SWA_GROUP + g]
                sg = jnp.where(valid, s[g * blk:(g + 1) * blk, :], -jnp.inf)
                mx = jnp.maximum(jnp.max(sg, axis=-1, keepdims=True), sink)
                e = jnp.exp(sg - mx)
                den = jnp.sum(e, axis=-1, keepdims=True) + jnp.exp(sink - mx)
                ps.append((e / den).astype(BF16))
            p = jnp.concatenate(ps, axis=0)
            o2 = jnp.dot(p, v2, preferred_element_type=F32)
            for pp in range(SWA_GROUP // 2):
                oa = o2[(2 * pp) * blk:(2 * pp + 1) * blk, :]
                ob = o2[(2 * pp + 1) * blk:(2 * pp + 2) * blk, :]
                pair = kh * (SWA_GROUP // 2) + pp
                o_ref[rows, pair * 128:(pair + 1) * 128] = jnp.where(low, oa, ob).astype(BF16)


def _swa_core(sinks, q, kv, batch, seq, tq):
    n = batch * seq
    steps = seq // tq
    bpt = tq // SWA_BLOCK
    tok = lambda c: pl.BlockSpec((tq, c), lambda b, i: (b * steps + i, 0))
    prev = pl.BlockSpec((SWA_BLOCK, 512),
                        lambda b, i: (jnp.maximum((b * steps + i) * bpt - 1, 0), 0))
    return pl.pallas_call(
        functools.partial(_swa_core_kernel, blocks=bpt),
        grid=(batch, steps),
        in_specs=[pl.BlockSpec(memory_space=pltpu.SMEM), tok(1024), tok(512), prev],
        out_specs=tok(1024),
        out_shape=jax.ShapeDtypeStruct((n, 1024), BF16),
        compiler_params=_params("parallel", "arbitrary"),
        name="swa_core",
    )(sinks, q, kv, kv)


def _out_kernel(x_ref, o_ref, w_ref, b_ref, g_ref, y_ref):
    m = jnp.dot(o_ref[...], w_ref[...], preferred_element_type=F32) + b_ref[...]
    y_ref[...] = x_ref[...] + _rms(m, g_ref[...])


def _out_proj(x, o, w, b, g, tm):
    n = x.shape[0]
    tok = lambda c: pl.BlockSpec((tm, c), lambda i: (i, 0))
    return pl.pallas_call(
        _out_kernel,
        grid=(n // tm,),
        in_specs=[tok(D_MODEL), tok(1024), _const_spec(w.shape), _const_spec(b.shape), _const_spec(g.shape)],
        out_specs=tok(D_MODEL),
        out_shape=jax.ShapeDtypeStruct((n, D_MODEL), F32),
        compiler_params=_params("parallel"),
        name="out_proj",
    )(x, o, w, b, g)


MLP_FF_CHUNK = 512


def _mlp_kernel(x_ref, g1_ref, g2_ref, wu_ref, wd_ref, y_ref, a_ref):
    x = x_ref[...]
    h = _rms(x, g1_ref[...]).astype(BF16)
    for j in range(D_FF // MLP_FF_CHUNK):
        cols = slice(j * MLP_FF_CHUNK, (j + 1) * MLP_FF_CHUNK)
        u = jnp.maximum(jnp.dot(h, wu_ref[:, cols], preferred_element_type=F32), 0.0)
        a_ref[:, cols] = (u * u).astype(BF16)
    f = jnp.dot(a_ref[...], wd_ref[...], preferred_element_type=F32)
    y_ref[...] = x + _rms(f, g2_ref[...])


def _mlp(x, g1, g2, wu, wd, tm):
    n = x.shape[0]
    tok = pl.BlockSpec((tm, D_MODEL), lambda i: (i, 0))
    return pl.pallas_call(
        _mlp_kernel,
        grid=(n // tm,),
        in_specs=[tok, _const_spec(g1.shape), _const_spec(g2.shape), _const_spec(wu.shape), _const_spec(wd.shape)],
        out_specs=tok,
        out_shape=jax.ShapeDtypeStruct((n, D_MODEL), F32),
        scratch_shapes=[pltpu.VMEM((tm, D_FF), BF16)],
        compiler_params=_params("parallel"),
        name="mlp",
    )(x, g1, g2, wu, wd)


def _split_bf16(w):
    hi = w.astype(BF16)
    return hi, (w - hi.astype(F32)).astype(BF16)


def _gla_weights(w_in, w_gate_up, b_gate_up):
    w_main = w_in[:, :3072].astype(BF16)
    w_lr = jnp.pad(w_in[:, 3072:], ((0, 0), (0, GLA_RANK_PAD - GLA_RANK))).astype(BF16)
    wg = jnp.pad(w_gate_up, ((0, GLA_RANK_PAD - GLA_RANK), (0, 0)))
    wg_hi, wg_lo = _split_bf16(wg)
    return w_main, w_lr, wg_hi, wg_lo, b_gate_up.reshape(1, GLA_DK)


def _swa_weights(w_in, b_in):
    scale = SWA_HD ** -0.5

    def arrange(t):
        q = t[..., :1024] * scale
        heads = [t[..., 1024 + i * SWA_HD:1024 + (i + 1) * SWA_HD] for i in range(2 * SWA_KVH)]
        return jnp.concatenate([q] + [h for h in heads for _ in range(2)], axis=-1)

    return arrange(w_in).astype(BF16), arrange(b_in).reshape(1, 1536)


def kernel(x, ln_mix_pre, ln_mix_post, ln_mlp_pre, ln_mlp_post, gla_w_in, gla_w_gate_up, gla_b_gate_up, gla_g_norm, gla_w_out, swa_w_in, swa_b_in, swa_sinks, swa_w_out, swa_b_out, mlp_w_up, mlp_w_down):
    batch, seq, d = x.shape
    n = batch * seq
    depth = ln_mix_pre.shape[0]
    tm = min(512, n)
    xf = x.reshape(n, d)
    zero_bias = jnp.zeros((1, D_MODEL), F32)
    row = lambda t: t.reshape(1, -1)
    for i in range(depth):
        j = i // 2
        if i % 2 == 0:
            w_main, w_lr, wg_hi, wg_lo, bg = _gla_weights(gla_w_in[j], gla_w_gate_up[j], gla_b_gate_up[j])
            qk, v, gate, la = _gla_in(xf, row(ln_mix_pre[i]), w_main, w_lr, wg_hi, wg_lo, bg, tm)
            o = _gla_core(qk, v, gate, la, row(gla_g_norm[j]), batch, seq, min(256, seq))
            w_out, b_out = gla_w_out[j].astype(BF16), zero_bias
        else:
            w, b = _swa_weights(swa_w_in[j], swa_b_in[j])
            q, kv = _swa_in(xf, row(ln_mix_pre[i]), w, b, tm)
            o = _swa_core(swa_sinks[j], q, kv, batch, seq, min(256, seq))
            w_out, b_out = swa_w_out[j].astype(BF16), row(swa_b_out[j])
        xf = _out_proj(xf, o, w_out, b_out, row(ln_mix_post[i]), tm)
        xf = _mlp(xf, row(ln_mlp_pre[i]), row(ln_mlp_post[i]),
                  mlp_w_up[i].astype(BF16), mlp_w_down[i].astype(BF16), tm)
    return xf.reshape(batch, seq, d)
```

```python
import functools
import math

import jax
import jax.numpy as jnp
from jax import lax
from jax.experimental import pallas as pl
from jax.experimental.pallas import tpu as pltpu

D_MODEL = 1024
NORM_EPS = 1e-6

GLA_HEADS = 4
GLA_DK = 512
GLA_DV = 1024
GLA_HK = 128
GLA_HV = 256
GLA_RANK = 16
GLA_RANK_PAD = 128
GLA_NORMALIZER = 16.0
GLA_CHUNK = 64

SWA_HD = 64
SWA_QH = 16
SWA_GROUP = 8
SWA_KVH = 2
SWA_BLOCK = 128
SWA_QT_ROWS = SWA_QH * SWA_HD + SWA_HD
SWA_ONES_ROWS = 16

D_FF = 4096

F32 = jnp.float32
BF16 = jnp.bfloat16

VMEM_LIMIT_BYTES = 56 * 1024 * 1024

NT_DIMS = (((1,), (1,)), ((), ()))
TN_DIMS = (((0,), (0,)), ((), ()))


def _rms(x, g):
    ms = jnp.mean(x * x, axis=-1, keepdims=True)
    return x * lax.rsqrt(ms + NORM_EPS) * g


def _const_spec(shape):
    nd = len(shape)
    return pl.BlockSpec(shape, lambda *_: (0,) * nd, pipeline_mode=pl.Buffered(1))


def _params(*sem):
    return pltpu.CompilerParams(dimension_semantics=sem, vmem_limit_bytes=VMEM_LIMIT_BYTES)


def _gla_in_kernel(x_ref, g_ref, w_ref, wlr_ref, wg2_ref, wg_lo_ref, bg_ref,
                   qk_ref, v_ref, gate_ref, la_ref):
    h = _rms(x_ref[...], g_ref[...]).astype(BF16)
    glr = jnp.dot(h, wlr_ref[...], preferred_element_type=F32)
    hi = glr.astype(BF16)
    lo = (glr - hi.astype(F32)).astype(BF16)
    z = (jnp.dot(jnp.concatenate([hi, lo], axis=1), wg2_ref[...], preferred_element_type=F32)
         + jnp.dot(hi, wg_lo_ref[...], preferred_element_type=F32)
         + bg_ref[...])
    la_ref[...] = jax.nn.log_sigmoid(z) / GLA_NORMALIZER
    for j, o_ref in enumerate((qk_ref, v_ref, gate_ref)):
        o_ref[...] = jnp.dot(h, w_ref[:, j * 1024:(j + 1) * 1024],
                             preferred_element_type=F32).astype(BF16)


def _gla_in(x, g, w, wlr, wg2, wg_lo, bg, tm):
    n = x.shape[0]
    tok = lambda c: pl.BlockSpec((tm, c), lambda i: (i, 0))
    return pl.pallas_call(
        _gla_in_kernel,
        grid=(n // tm,),
        in_specs=[tok(D_MODEL), _const_spec(g.shape), _const_spec(w.shape), _const_spec(wlr.shape),
                  _const_spec(wg2.shape), _const_spec(wg_lo.shape), _const_spec(bg.shape)],
        out_specs=[tok(1024), tok(1024), tok(1024), tok(GLA_DK)],
        out_shape=[jax.ShapeDtypeStruct((n, 1024), BF16)] * 3 + [jax.ShapeDtypeStruct((n, GLA_DK), F32)],
        compiler_params=_params("parallel"),
        name="gla_in",
    )(x, g, w, wlr, wg2, wg_lo, bg)


def _gla_core_kernel(qk_ref, v_ref, gate_ref, la_ref, gn_ref, o_ref, st_ref, *, chunks):
    @pl.when(pl.program_id(1) == 0)
    def _():
        st_ref[...] = jnp.zeros_like(st_ref)

    c = GLA_CHUNK
    row = lax.broadcasted_iota(jnp.int32, (c, c), 0)
    col = lax.broadcasted_iota(jnp.int32, (c, c), 1)
    causal = col <= row
    tril = jnp.where(causal, 1.0, 0.0).astype(BF16)
    scale = GLA_HK ** -0.5
    gn = gn_ref[...]

    for ci in range(chunks):
        rows = slice(ci * c, (ci + 1) * c)
        la = la_ref[rows, :]
        la_hi = la.astype(BF16)
        la_lo = (la - la_hi.astype(F32)).astype(BF16)
        bc = (jnp.dot(tril, la_hi, preferred_element_type=F32)
              + jnp.dot(tril, la_lo, preferred_element_type=F32))
        b_last = bc[c - 1:c, :]
        q = qk_ref[rows, :GLA_DK].astype(F32)
        k = qk_ref[rows, GLA_DK:].astype(F32)
        q_dec = (q * (scale * jnp.exp(bc))).astype(BF16)
        k_inv = (k * jnp.exp(-bc)).astype(BF16)
        k_end = (k * jnp.exp(b_last - bc)).astype(BF16)
        decay = jnp.exp(b_last)
        for h in range(GLA_HEADS):
            ks = slice(h * GLA_HK, (h + 1) * GLA_HK)
            vs = slice(h * GLA_HV, (h + 1) * GLA_HV)
            v = v_ref[rows, vs]
            att = lax.dot_general(q_dec[:, ks], k_inv[:, ks], NT_DIMS,
                                  preferred_element_type=F32)
            att = jnp.where(causal, att, 0.0).astype(BF16)
            st = st_ref[h]
            o = (jnp.dot(att, v, preferred_element_type=F32)
                 + lax.dot_general(q_dec[:, ks], st.astype(BF16), NT_DIMS,
                                   preferred_element_type=F32))
            v_t = v.astype(F32).T.astype(BF16)
            st_ref[h] = st * decay[:, ks] + jnp.dot(v_t, k_end[:, ks], preferred_element_type=F32)
            o = _rms(o, gn)
            gate = gate_ref[rows, vs].astype(F32)
            o_ref[rows, vs] = (o * (gate * jax.nn.sigmoid(gate))).astype(BF16)


def _gla_core(qk, v, gate, la, gn, batch, seq, tc):
    n = batch * seq
    steps = seq // tc
    tok = lambda c: pl.BlockSpec((tc, c), lambda b, i: (b * steps + i, 0))
    return pl.pallas_call(
        functools.partial(_gla_core_kernel, chunks=tc // GLA_CHUNK),
        grid=(batch, steps),
        in_specs=[tok(1024), tok(1024), tok(1024), tok(GLA_DK), _const_spec(gn.shape)],
        out_specs=tok(1024),
        out_shape=jax.ShapeDtypeStruct((n, GLA_DV), BF16),
        scratch_shapes=[pltpu.VMEM((GLA_HEADS, GLA_HV, GLA_HK), F32)],
        compiler_params=_params("parallel", "arbitrary"),
        name="gla_core",
    )(qk, v, gate, la, gn)


def _swa_in_kernel(x_ref, g_ref, wqt_ref, bqt_ref, wk_ref, bk_ref, wvt_ref, bvt_ref,
                   qt_ref, kp_ref, vt_ref):
    h = _rms(x_ref[...], g_ref[...]).astype(BF16)
    kp_ref[...] = (jnp.dot(h, wk_ref[...], preferred_element_type=F32) + bk_ref[...]).astype(BF16)
    vt_ref[...] = (lax.dot_general(wvt_ref[...], h, NT_DIMS, preferred_element_type=F32)
                   + bvt_ref[...]).astype(BF16)
    qt_ref[...] = (lax.dot_general(wqt_ref[...], h, NT_DIMS, preferred_element_type=F32)
                   + bqt_ref[...]).astype(BF16)


def _swa_in(x, g, wqt, bqt, wk, bk, wvt, bvt, tm):
    n = x.shape[0]
    consts = (g, wqt, bqt, wk, bk, wvt, bvt)
    return pl.pallas_call(
        _swa_in_kernel,
        grid=(n // tm,),
        in_specs=[pl.BlockSpec((tm, D_MODEL), lambda i: (i, 0))] + [_const_spec(c.shape) for c in consts],
        out_specs=[pl.BlockSpec((SWA_QT_ROWS, tm), lambda i: (0, i)),
                   pl.BlockSpec((tm, 256), lambda i: (i, 0)),
                   pl.BlockSpec((128, tm), lambda i: (0, i))],
        out_shape=[jax.ShapeDtypeStruct((SWA_QT_ROWS, n), BF16),
                   jax.ShapeDtypeStruct((n, 256), BF16),
                   jax.ShapeDtypeStruct((128, n), BF16)],
        compiler_params=_params("parallel"),
        name="swa_in",
    )(x, *consts)


def _swa_core_kernel(sink_ref, qt_ref, kp_ref, kpp_ref, vt_ref, vtp_ref, ot_ref, *, blocks):
    first_tile = pl.program_id(1) == 0
    blk = SWA_BLOCK
    sj = lax.broadcasted_iota(jnp.int32, (2 * blk, blk), 0)
    qi = lax.broadcasted_iota(jnp.int32, (2 * blk, blk), 1)
    band = (sj > qi) & (sj <= qi + blk)
    ones = jnp.ones((SWA_ONES_ROWS, 2 * blk), BF16)
    log2e = math.log2(math.e)

    def heads_of(kh):
        return [kh * SWA_GROUP + g for g in range(SWA_GROUP)]

    def scores(j, kh):
        cols = slice(j * blk, (j + 1) * blk)
        k_prev = kpp_ref[:, kh * 128:(kh + 1) * 128] if j == 0 else kp_ref[(j - 1) * blk:j * blk, kh * 128:(kh + 1) * 128]
        k_band = jnp.concatenate([k_prev, kp_ref[cols, kh * 128:(kh + 1) * 128]], axis=0)
        qt = jnp.concatenate([qt_ref[hq * SWA_HD:hq * SWA_HD + 2 * SWA_HD, cols] for hq in heads_of(kh)], axis=1)
        return jnp.dot(k_band, qt, preferred_element_type=F32)

    def finish(j, kh, s):
        cols = slice(j * blk, (j + 1) * blk)
        valid = band & (jnp.logical_not(first_tile) | (sj >= blk)) if j == 0 else band
        es, mxs = [], []
        for g, hq in enumerate(heads_of(kh)):
            sink = sink_ref[hq] * log2e
            sg = jnp.where(valid, s[:, g * blk:(g + 1) * blk], -jnp.inf)
            mx = jnp.maximum(jnp.max(sg, axis=0, keepdims=True), sink)
            es.append(jnp.exp2(sg - mx).astype(BF16))
            mxs.append(jnp.exp2(sink - mx))
        e = jnp.concatenate(es, axis=1)
        rows = slice(kh * SWA_HD, (kh + 1) * SWA_HD)
        vt_prev = vtp_ref[rows, :] if j == 0 else vt_ref[rows, (j - 1) * blk:j * blk]
        lhs = jnp.concatenate([jnp.concatenate([vt_prev, vt_ref[rows, cols]], axis=1), ones], axis=0)
        o_aug = jnp.dot(lhs, e, preferred_element_type=F32)
        den = o_aug[SWA_HD:SWA_HD + 1, :] + jnp.concatenate(mxs, axis=1)
        o = (o_aug[:SWA_HD, :] / den).astype(BF16)
        for g, hq in enumerate(heads_of(kh)):
            ot_ref[hq * SWA_HD:(hq + 1) * SWA_HD, cols] = o[:, g * blk:(g + 1) * blk]

    units = [(j, kh) for j in range(blocks) for kh in range(SWA_KVH)]
    s_next = scores(*units[0])
    for idx, unit in enumerate(units):
        s = s_next
        if idx + 1 < len(units):
            s_next = scores(*units[idx + 1])
        finish(*unit, s)


def _swa_core(sinks, qt, kp, vt, batch, seq, tq):
    n = batch * seq
    steps = seq // tq
    bpt = tq // SWA_BLOCK
    tile = lambda b, i: b * steps + i
    prev_blk = lambda b, i: jnp.maximum(tile(b, i) * bpt - 1, 0)
    return pl.pallas_call(
        functools.partial(_swa_core_kernel, blocks=bpt),
        grid=(batch, steps),
        in_specs=[pl.BlockSpec(memory_space=pltpu.SMEM),
                  pl.BlockSpec((SWA_QT_ROWS, tq), lambda b, i: (0, tile(b, i))),
                  pl.BlockSpec((tq, 256), lambda b, i: (tile(b, i), 0)),
                  pl.BlockSpec((SWA_BLOCK, 256), lambda b, i: (prev_blk(b, i), 0)),
                  pl.BlockSpec((128, tq), lambda b, i: (0, tile(b, i))),
                  pl.BlockSpec((128, SWA_BLOCK), lambda b, i: (0, prev_blk(b, i)))],
        out_specs=pl.BlockSpec((SWA_QH * SWA_HD, tq), lambda b, i: (0, tile(b, i))),
        out_shape=jax.ShapeDtypeStruct((SWA_QH * SWA_HD, n), BF16),
        compiler_params=_params("parallel", "arbitrary"),
        name="swa_core",
    )(sinks, qt, kp, kp, vt, vt)


MLP_FF_CHUNK = 512


def _dense_kernel(x_ref, o_ref, wo_ref, bo_ref, gpost_ref, g1_ref, g2_ref, wu_ref, wd_ref,
                  y_ref, a_ref, *, o_transposed):
    dims = TN_DIMS if o_transposed else (((1,), (0,)), ((), ()))
    m = lax.dot_general(o_ref[...], wo_ref[...], dims, preferred_element_type=F32) + bo_ref[...]
    x = x_ref[...] + _rms(m, gpost_ref[...])
    h = _rms(x, g1_ref[...]).astype(BF16)
    for j in range(D_FF // MLP_FF_CHUNK):
        cols = slice(j * MLP_FF_CHUNK, (j + 1) * MLP_FF_CHUNK)
        u = jnp.maximum(jnp.dot(h, wu_ref[:, cols], preferred_element_type=F32), 0.0)
        a_ref[:, cols] = (u * u).astype(BF16)
    f = jnp.dot(a_ref[...], wd_ref[...], preferred_element_type=F32)
    y_ref[...] = x + _rms(f, g2_ref[...])


def _dense(x, o, wo, bo, gpost, g1, g2, wu, wd, tm, o_transposed):
    n = x.shape[0]
    tok = pl.BlockSpec((tm, D_MODEL), lambda i: (i, 0))
    o_spec = pl.BlockSpec((1024, tm), lambda i: (0, i)) if o_transposed else tok
    consts = (wo, bo, gpost, g1, g2, wu, wd)
    return pl.pallas_call(
        functools.partial(_dense_kernel, o_transposed=o_transposed),
        grid=(n // tm,),
        in_specs=[tok, o_spec] + [_const_spec(c.shape) for c in consts],
        out_specs=tok,
        out_shape=jax.ShapeDtypeStruct((n, D_MODEL), F32),
        scratch_shapes=[pltpu.VMEM((tm, D_FF), BF16)],
        compiler_params=_params("parallel"),
        name="dense",
    )(x, o, *consts)


def _split_bf16(w):
    hi = w.astype(BF16)
    return hi, (w - hi.astype(F32)).astype(BF16)


def _gla_weights(w_in, w_gate_up, b_gate_up):
    w_main = w_in[:, :3072].astype(BF16)
    w_lr = jnp.pad(w_in[:, 3072:], ((0, 0), (0, GLA_RANK_PAD - GLA_RANK))).astype(BF16)
    wg = jnp.pad(w_gate_up, ((0, GLA_RANK_PAD - GLA_RANK), (0, 0)))
    wg_hi, wg_lo = _split_bf16(wg)
    return w_main, w_lr, jnp.concatenate([wg_hi, wg_hi], axis=0), wg_lo, b_gate_up.reshape(1, GLA_DK)


def _swa_weights(w_in, b_in):
    nq = SWA_QH * SWA_HD
    qscale = SWA_HD ** -0.5 * math.log2(math.e)
    pad_q = SWA_QT_ROWS - nq
    wqt = jnp.pad((w_in[:, :nq] * qscale).T, ((0, pad_q), (0, 0))).astype(BF16)
    bqt = jnp.pad(b_in[:nq] * qscale, (0, pad_q)).reshape(SWA_QT_ROWS, 1)

    def pad_heads(t):
        z = jnp.zeros_like(t[..., :SWA_HD])
        return jnp.concatenate([t[..., :SWA_HD], z, t[..., SWA_HD:], z], axis=-1)

    wk = pad_heads(w_in[:, nq:nq + 128]).astype(BF16)
    bk = pad_heads(b_in[nq:nq + 128]).reshape(1, 256)
    wvt = w_in[:, nq + 128:].T.astype(BF16)
    bvt = b_in[nq + 128:].reshape(128, 1)
    return wqt, bqt, wk, bk, wvt, bvt


def kernel(x, ln_mix_pre, ln_mix_post, ln_mlp_pre, ln_mlp_post, gla_w_in, gla_w_gate_up, gla_b_gate_up, gla_g_norm, gla_w_out, swa_w_in, swa_b_in, swa_sinks, swa_w_out, swa_b_out, mlp_w_up, mlp_w_down):
    batch, seq, d = x.shape
    n = batch * seq
    depth = ln_mix_pre.shape[0]
    tm = min(512, n)
    xf = x.reshape(n, d)
    zero_bias = jnp.zeros((1, D_MODEL), F32)
    row = lambda t: t.reshape(1, -1)
    for i in range(depth):
        j = i // 2
        if i % 2 == 0:
            w_main, w_lr, wg2, wg_lo, bg = _gla_weights(gla_w_in[j], gla_w_gate_up[j], gla_b_gate_up[j])
            qk, v, gate, la = _gla_in(xf, row(ln_mix_pre[i]), w_main, w_lr, wg2, wg_lo, bg, tm)
            o = _gla_core(qk, v, gate, la, row(gla_g_norm[j]), batch, seq, min(256, seq))
            w_out, b_out = gla_w_out[j].astype(BF16), zero_bias
        else:
            qt, kp, vt = _swa_in(xf, row(ln_mix_pre[i]), *_swa_weights(swa_w_in[j], swa_b_in[j]), tm)
            o = _swa_core(swa_sinks[j], qt, kp, vt, batch, seq, min(512, seq))
            w_out, b_out = swa_w_out[j].astype(BF16), row(swa_b_out[j])
        xf = _dense(xf, o, w_out, b_out, row(ln_mix_post[i]), row(ln_mlp_pre[i]), row(ln_mlp_post[i]),
                    mlp_w_up[i].astype(BF16), mlp_w_down[i].astype(BF16), tm, o_transposed=(i % 2 == 1))
    return xf.reshape(batch, seq, d)
```

```python
import functools
import math

import jax
import jax.numpy as jnp
from jax import lax
from jax.experimental import pallas as pl
from jax.experimental.pallas import tpu as pltpu

D_MODEL = 1024
NORM_EPS = 1e-6

GLA_HEADS = 4
GLA_DK = 512
GLA_DV = 1024
GLA_HK = 128
GLA_HV = 256
GLA_RANK = 16
GLA_RANK_PAD = 128
GLA_NORMALIZER = 16.0
GLA_CHUNK = 64

SWA_HD = 64
SWA_QH = 16
SWA_GROUP = 8
SWA_KVH = 2
SWA_BLOCK = 128
SWA_QT_ROWS = SWA_QH * SWA_HD + SWA_HD
SWA_ONES_ROWS = 16

D_FF = 4096

F32 = jnp.float32
BF16 = jnp.bfloat16

VMEM_LIMIT_BYTES = 56 * 1024 * 1024

NT_DIMS = (((1,), (1,)), ((), ()))
TN_DIMS = (((0,), (0,)), ((), ()))


def _rms(x, g):
    ms = jnp.mean(x * x, axis=-1, keepdims=True)
    return x * lax.rsqrt(ms + NORM_EPS) * g


def _layer_spec(stacked, layer):
    nd = stacked.ndim - 1
    return pl.BlockSpec((None,) + stacked.shape[1:], lambda *_: (layer,) + (0,) * nd,
                        pipeline_mode=pl.Buffered(1))


def _params(*sem):
    return pltpu.CompilerParams(dimension_semantics=sem, vmem_limit_bytes=VMEM_LIMIT_BYTES)


def _gla_in_kernel(x_ref, g_ref, w_ref, wlrt_ref, wg3_ref, bg_ref, gn_ref,
                   qd_ref, ki_ref, ke_ref, v_ref, sg_ref, dec_ref, *, chunks):
    c = GLA_CHUNK
    h = _rms(x_ref[...], g_ref[...]).astype(BF16)
    glr_t = lax.dot_general(wlrt_ref[...], h, NT_DIMS, preferred_element_type=F32)
    hi = glr_t.astype(BF16)
    lo = (glr_t - hi.astype(F32)).astype(BF16)
    z = lax.dot_general(jnp.concatenate([hi, lo, hi], axis=0), wg3_ref[...], TN_DIMS,
                        preferred_element_type=F32) + bg_ref[...]
    la = (jnp.minimum(z, 0.0) - jnp.log(1.0 + jnp.exp(-jnp.abs(z)))) * (1.0 / GLA_NORMALIZER)
    la_hi = la.astype(BF16)
    la_lo = (la - la_hi.astype(F32)).astype(BF16)

    row = lax.broadcasted_iota(jnp.int32, (c, 2 * c), 0)
    col = lax.broadcasted_iota(jnp.int32, (c, 2 * c), 1)
    tril2 = jnp.where((col % c) <= row, 1.0, 0.0).astype(BF16)

    qk = jnp.dot(h, w_ref[:, :1024], preferred_element_type=F32)
    gate = jnp.dot(h, w_ref[:, 2048:], preferred_element_type=F32)
    sg_ref[...] = (gate * jax.nn.sigmoid(gate) * gn_ref[...]).astype(BF16)
    scale = GLA_HK ** -0.5
    for ci in range(chunks):
        rows = slice(ci * c, (ci + 1) * c)
        bc = jnp.dot(tril2, jnp.concatenate([la_hi[rows], la_lo[rows]], axis=0),
                     preferred_element_type=F32)
        b_last = bc[c - 1:c, :]
        q = qk[rows, :GLA_DK]
        k = qk[rows, GLA_DK:]
        qd_ref[rows, :] = (q * (scale * jnp.exp(bc))).astype(BF16)
        ki_ref[rows, :] = (k * jnp.exp(-bc)).astype(BF16)
        ke_ref[rows, :] = (k * jnp.exp(b_last - bc)).astype(BF16)
        dec_ref[ci:ci + 1, :] = jnp.exp(b_last)
    v_ref[...] = jnp.dot(h, w_ref[:, 1024:2048], preferred_element_type=F32).astype(BF16)


def _gla_in(x, g, w, wlrt, wg3, bg, gn, layer, j, tm):
    n = x.shape[0]
    chunks = tm // GLA_CHUNK
    tok = lambda c: pl.BlockSpec((tm, c), lambda i: (i, 0))
    return pl.pallas_call(
        functools.partial(_gla_in_kernel, chunks=chunks),
        grid=(n // tm,),
        in_specs=[tok(D_MODEL), _layer_spec(g, layer), _layer_spec(w, j), _layer_spec(wlrt, j),
                  _layer_spec(wg3, j), _layer_spec(bg, j), _layer_spec(gn, j)],
        out_specs=[tok(GLA_DK), tok(GLA_DK), tok(GLA_DK), tok(1024), tok(1024),
                   pl.BlockSpec((chunks, GLA_DK), lambda i: (i, 0))],
        out_shape=[jax.ShapeDtypeStruct((n, GLA_DK), BF16)] * 3 + [jax.ShapeDtypeStruct((n, 1024), BF16)] * 2
                  + [jax.ShapeDtypeStruct((n // GLA_CHUNK, GLA_DK), F32)],
        compiler_params=_params("parallel"),
        name="gla_in",
    )(x, g, w, wlrt, wg3, bg, gn)


def _gla_core_kernel(qd_ref, ki_ref, ke_ref, v_ref, sg_ref, dec_ref, o_ref, st_ref, *, chunks):
    @pl.when(pl.program_id(1) == 0)
    def _():
        st_ref[...] = jnp.zeros_like(st_ref)

    c = GLA_CHUNK
    row = lax.broadcasted_iota(jnp.int32, (c, c), 0)
    col = lax.broadcasted_iota(jnp.int32, (c, c), 1)
    causal = col <= row

    def scores(ci):
        rows = slice(ci * c, (ci + 1) * c)
        return [lax.dot_general(qd_ref[rows, h * GLA_HK:(h + 1) * GLA_HK],
                                ki_ref[rows, h * GLA_HK:(h + 1) * GLA_HK], NT_DIMS,
                                preferred_element_type=F32) for h in range(GLA_HEADS)]

    att_next = scores(0)
    for ci in range(chunks):
        rows = slice(ci * c, (ci + 1) * c)
        att_all = att_next
        if ci + 1 < chunks:
            att_next = scores(ci + 1)
        decay = dec_ref[ci:ci + 1, :]
        for h in range(GLA_HEADS):
            ks = slice(h * GLA_HK, (h + 1) * GLA_HK)
            vs = slice(h * GLA_HV, (h + 1) * GLA_HV)
            v = v_ref[rows, vs]
            att = jnp.where(causal, att_all[h], 0.0).astype(BF16)
            st = st_ref[h]
            o = (jnp.dot(att, v, preferred_element_type=F32)
                 + lax.dot_general(qd_ref[rows, ks], st.astype(BF16), NT_DIMS,
                                   preferred_element_type=F32))
            st_ref[h] = st * decay[:, ks] + lax.dot_general(v, ke_ref[rows, ks], TN_DIMS,
                                                            preferred_element_type=F32)
            ms = jnp.mean(o * o, axis=-1, keepdims=True)
            o_ref[rows, vs] = (o * lax.rsqrt(ms + NORM_EPS) * sg_ref[rows, vs].astype(F32)).astype(BF16)


def _gla_core(qd, ki, ke, v, sg, dec, batch, seq, tc):
    n = batch * seq
    steps = seq // tc
    chunks = tc // GLA_CHUNK
    tok = lambda c: pl.BlockSpec((tc, c), lambda b, i: (b * steps + i, 0))
    return pl.pallas_call(
        functools.partial(_gla_core_kernel, chunks=chunks),
        grid=(batch, steps),
        in_specs=[tok(GLA_DK), tok(GLA_DK), tok(GLA_DK), tok(1024), tok(1024),
                  pl.BlockSpec((chunks, GLA_DK), lambda b, i: (b * steps + i, 0))],
        out_specs=tok(1024),
        out_shape=jax.ShapeDtypeStruct((n, GLA_DV), BF16),
        scratch_shapes=[pltpu.VMEM((GLA_HEADS, GLA_HV, GLA_HK), F32)],
        compiler_params=_params("parallel", "arbitrary"),
        name="gla_core",
    )(qd, ki, ke, v, sg, dec)


def _swa_in_kernel(x_ref, g_ref, wqt_ref, bqt_ref, wk_ref, bk_ref, wvt_ref, bvt_ref,
                   qt_ref, kp_ref, vt_ref):
    h = _rms(x_ref[...], g_ref[...]).astype(BF16)
    kp_ref[...] = (jnp.dot(h, wk_ref[...], preferred_element_type=F32) + bk_ref[...]).astype(BF16)
    vt_ref[...] = (lax.dot_general(wvt_ref[...], h, NT_DIMS, preferred_element_type=F32)
                   + bvt_ref[...]).astype(BF16)
    qt_ref[...] = (lax.dot_general(wqt_ref[...], h, NT_DIMS, preferred_element_type=F32)
                   + bqt_ref[...]).astype(BF16)


def _swa_in(x, g, wqt, bqt, wk, bk, wvt, bvt, layer, j, tm):
    n = x.shape[0]
    consts = (wqt, bqt, wk, bk, wvt, bvt)
    return pl.pallas_call(
        _swa_in_kernel,
        grid=(n // tm,),
        in_specs=[pl.BlockSpec((tm, D_MODEL), lambda i: (i, 0)), _layer_spec(g, layer)]
                 + [_layer_spec(c, j) for c in consts],
        out_specs=[pl.BlockSpec((SWA_QT_ROWS, tm), lambda i: (0, i)),
                   pl.BlockSpec((tm, 256), lambda i: (i, 0)),
                   pl.BlockSpec((128, tm), lambda i: (0, i))],
        out_shape=[jax.ShapeDtypeStruct((SWA_QT_ROWS, n), BF16),
                   jax.ShapeDtypeStruct((n, 256), BF16),
                   jax.ShapeDtypeStruct((128, n), BF16)],
        compiler_params=_params("parallel"),
        name="swa_in",
    )(x, g, *consts)


def _swa_core_kernel(sink_ref, qt_ref, kp_ref, kpp_ref, vt_ref, vtp_ref, ot_ref, *, blocks):
    first_tile = pl.program_id(1) == 0
    blk = SWA_BLOCK
    sj = lax.broadcasted_iota(jnp.int32, (2 * blk, blk), 0)
    qi = lax.broadcasted_iota(jnp.int32, (2 * blk, blk), 1)
    band = (sj > qi) & (sj <= qi + blk)
    ones = jnp.ones((SWA_ONES_ROWS, 2 * blk), BF16)
    log2e = math.log2(math.e)

    def heads_of(kh):
        return [kh * SWA_GROUP + g for g in range(SWA_GROUP)]

    def scores(j, kh):
        cols = slice(j * blk, (j + 1) * blk)
        k_prev = kpp_ref[:, kh * 128:(kh + 1) * 128] if j == 0 else kp_ref[(j - 1) * blk:j * blk, kh * 128:(kh + 1) * 128]
        k_band = jnp.concatenate([k_prev, kp_ref[cols, kh * 128:(kh + 1) * 128]], axis=0)
        qt = jnp.concatenate([qt_ref[hq * SWA_HD:hq * SWA_HD + 2 * SWA_HD, cols] for hq in heads_of(kh)], axis=1)
        return jnp.dot(k_band, qt, preferred_element_type=F32)

    def finish(j, kh, s):
        cols = slice(j * blk, (j + 1) * blk)
        valid = band & (jnp.logical_not(first_tile) | (sj >= blk)) if j == 0 else band
        es, mxs = [], []
        for g, hq in enumerate(heads_of(kh)):
            sink = sink_ref[hq] * log2e
            sg = jnp.where(valid, s[:, g * blk:(g + 1) * blk], -jnp.inf)
            mx = jnp.maximum(jnp.max(sg, axis=0, keepdims=True), sink)
            es.append(jnp.exp2(sg - mx).astype(BF16))
            mxs.append(jnp.exp2(sink - mx))
        e = jnp.concatenate(es, axis=1)
        rows = slice(kh * SWA_HD, (kh + 1) * SWA_HD)
        vt_prev = vtp_ref[rows, :] if j == 0 else vt_ref[rows, (j - 1) * blk:j * blk]
        lhs = jnp.concatenate([jnp.concatenate([vt_prev, vt_ref[rows, cols]], axis=1), ones], axis=0)
        o_aug = jnp.dot(lhs, e, preferred_element_type=F32)
        den = o_aug[SWA_HD:SWA_HD + 1, :] + jnp.concatenate(mxs, axis=1)
        o = (o_aug[:SWA_HD, :] / den).astype(BF16)
        for g, hq in enumerate(heads_of(kh)):
            ot_ref[hq * SWA_HD:(hq + 1) * SWA_HD, cols] = o[:, g * blk:(g + 1) * blk]

    units = [(j, kh) for j in range(blocks) for kh in range(SWA_KVH)]
    s_next = scores(*units[0])
    for idx, unit in enumerate(units):
        s = s_next
        if idx + 1 < len(units):
            s_next = scores(*units[idx + 1])
        finish(*unit, s)


def _swa_core(sinks, qt, kp, vt, batch, seq, tq):
    n = batch * seq
    steps = seq // tq
    bpt = tq // SWA_BLOCK
    tile = lambda b, i: b * steps + i
    prev_blk = lambda b, i: jnp.maximum(tile(b, i) * bpt - 1, 0)
    return pl.pallas_call(
        functools.partial(_swa_core_kernel, blocks=bpt),
        grid=(batch, steps),
        in_specs=[pl.BlockSpec(memory_space=pltpu.SMEM),
                  pl.BlockSpec((SWA_QT_ROWS, tq), lambda b, i: (0, tile(b, i))),
                  pl.BlockSpec((tq, 256), lambda b, i: (tile(b, i), 0)),
                  pl.BlockSpec((SWA_BLOCK, 256), lambda b, i: (prev_blk(b, i), 0)),
                  pl.BlockSpec((128, tq), lambda b, i: (0, tile(b, i))),
                  pl.BlockSpec((128, SWA_BLOCK), lambda b, i: (0, prev_blk(b, i)))],
        out_specs=pl.BlockSpec((SWA_QH * SWA_HD, tq), lambda b, i: (0, tile(b, i))),
        out_shape=jax.ShapeDtypeStruct((SWA_QH * SWA_HD, n), BF16),
        compiler_params=_params("parallel", "arbitrary"),
        name="swa_core",
    )(sinks, qt, kp, kp, vt, vt)


MLP_FF_CHUNK = 512


def _dense_kernel(x_ref, o_ref, wo_ref, bo_ref, gpost_ref, g1_ref, g2_ref, wu_ref, wd_ref,
                  y_ref, a_ref, *, o_transposed):
    dims = TN_DIMS if o_transposed else (((1,), (0,)), ((), ()))
    tm = x_ref.shape[0]
    halves = [slice(0, tm // 2), slice(tm // 2, tm)]
    xs, hs = [], []
    for r in halves:
        o = o_ref[:, r] if o_transposed else o_ref[r, :]
        m = lax.dot_general(o, wo_ref[...], dims, preferred_element_type=F32) + bo_ref[...]
        x = x_ref[r, :] + _rms(m, gpost_ref[...])
        xs.append(x)
        hs.append(_rms(x, g1_ref[...]).astype(BF16))
    h = jnp.concatenate(hs, axis=0)
    for j in range(D_FF // MLP_FF_CHUNK):
        cols = slice(j * MLP_FF_CHUNK, (j + 1) * MLP_FF_CHUNK)
        u = jnp.maximum(jnp.dot(h, wu_ref[:, cols], preferred_element_type=F32), 0.0)
        a_ref[:, cols] = (u * u).astype(BF16)
    for r, x in zip(halves, xs):
        f = jnp.dot(a_ref[r, :], wd_ref[...], preferred_element_type=F32)
        y_ref[r, :] = x + _rms(f, g2_ref[...])


def _dense(x, o, wo, bo, gpost, g1, g2, wu, wd, layer, j, tm, o_transposed):
    n = x.shape[0]
    tok = pl.BlockSpec((tm, D_MODEL), lambda i: (i, 0))
    o_spec = pl.BlockSpec((1024, tm), lambda i: (0, i)) if o_transposed else tok
    return pl.pallas_call(
        functools.partial(_dense_kernel, o_transposed=o_transposed),
        grid=(n // tm,),
        in_specs=[tok, o_spec, _layer_spec(wo, j), _layer_spec(bo, j), _layer_spec(gpost, layer),
                  _layer_spec(g1, layer), _layer_spec(g2, layer), _layer_spec(wu, layer), _layer_spec(wd, layer)],
        out_specs=tok,
        out_shape=jax.ShapeDtypeStruct((n, D_MODEL), F32),
        scratch_shapes=[pltpu.VMEM((tm, D_FF), BF16)],
        compiler_params=_params("parallel"),
        name="dense",
    )(x, o, wo, bo, gpost, g1, g2, wu, wd)


def _split_bf16(w):
    hi = w.astype(BF16)
    return hi, (w - hi.astype(F32)).astype(BF16)


def _gla_weights(w_in, w_gate_up, b_gate_up, g_norm):
    layers = w_in.shape[0]
    w_main = w_in[:, :, :3072].astype(BF16)
    w_lr_t = jnp.swapaxes(w_in[:, :, 3072:], 1, 2).astype(BF16)
    wg_hi, wg_lo = _split_bf16(w_gate_up)
    gn = jnp.tile(g_norm, (1, GLA_HEADS)).reshape(layers, 1, GLA_DV)
    return (w_main, w_lr_t, jnp.concatenate([wg_hi, wg_hi, wg_lo], axis=1),
            b_gate_up.reshape(layers, 1, GLA_DK), gn)


def _swa_weights(w_in, b_in):
    layers = w_in.shape[0]
    nq = SWA_QH * SWA_HD
    qscale = SWA_HD ** -0.5 * math.log2(math.e)
    pad_q = SWA_QT_ROWS - nq
    wqt = jnp.pad(jnp.swapaxes(w_in[:, :, :nq] * qscale, 1, 2), ((0, 0), (0, pad_q), (0, 0))).astype(BF16)
    bqt = jnp.pad(b_in[:, :nq] * qscale, ((0, 0), (0, pad_q))).reshape(layers, SWA_QT_ROWS, 1)

    def pad_heads(t):
        z = jnp.zeros_like(t[..., :SWA_HD])
        return jnp.concatenate([t[..., :SWA_HD], z, t[..., SWA_HD:], z], axis=-1)

    wk = pad_heads(w_in[:, :, nq:nq + 128]).astype(BF16)
    bk = pad_heads(b_in[:, nq:nq + 128]).reshape(layers, 1, 256)
    wvt = jnp.swapaxes(w_in[:, :, nq + 128:], 1, 2).astype(BF16)
    bvt = b_in[:, nq + 128:].reshape(layers, 128, 1)
    return wqt, bqt, wk, bk, wvt, bvt


def kernel(x, ln_mix_pre, ln_mix_post, ln_mlp_pre, ln_mlp_post, gla_w_in, gla_w_gate_up, gla_b_gate_up, gla_g_norm, gla_w_out, swa_w_in, swa_b_in, swa_sinks, swa_w_out, swa_b_out, mlp_w_up, mlp_w_down):
    batch, seq, d = x.shape
    n = batch * seq
    depth = ln_mix_pre.shape[0]
    tm = min(512, n)
    xf = x.reshape(n, d)
    gain = lambda t: t.reshape(t.shape[0], 1, t.shape[1])
    g_pre, g_post, g1, g2 = gain(ln_mix_pre), gain(ln_mix_post), gain(ln_mlp_pre), gain(ln_mlp_post)
    gla_w = _gla_weights(gla_w_in, gla_w_gate_up, gla_b_gate_up, gla_g_norm)
    swa_w = _swa_weights(swa_w_in, swa_b_in)
    gla_wo, swa_wo = gla_w_out.astype(BF16), swa_w_out.astype(BF16)
    gla_bo = jnp.zeros((gla_w_out.shape[0], 1, D_MODEL), F32)
    swa_bo = gain(swa_b_out)
    wu, wd = mlp_w_up.astype(BF16), mlp_w_down.astype(BF16)
    for i in range(depth):
        j = i // 2
        if i % 2 == 0:
            qd, ki, ke, v, sg, dec = _gla_in(xf, g_pre, *gla_w, i, j, tm)
            o = _gla_core(qd, ki, ke, v, sg, dec, batch, seq, min(512, seq))
            wo, bo = gla_wo, gla_bo
        else:
            qt, kp, vt = _swa_in(xf, g_pre, *swa_w, i, j, tm)
            o = _swa_core(swa_sinks[j], qt, kp, vt, batch, seq, min(512, seq))
            wo, bo = swa_wo, swa_bo
        xf = _dense(xf, o, wo, bo, g_post, g1, g2, wu, wd, i, j, tm, o_transposed=(i % 2 == 1))
    return xf.reshape(batch, seq, d)
```

```python
import functools
import math

import jax
import jax.numpy as jnp
from jax import lax
from jax.experimental import pallas as pl
from jax.experimental.pallas import tpu as pltpu

D_MODEL = 1024
NORM_EPS = 1e-6

GLA_HEADS = 4
GLA_DK = 512
GLA_DV = 1024
GLA_HK = 128
GLA_HV = 256
GLA_RANK = 16
GLA_RANK_PAD = 128
GLA_NORMALIZER = 16.0
GLA_CHUNK = 64

SWA_HD = 64
SWA_QH = 16
SWA_GROUP = 8
SWA_KVH = 2
SWA_BLOCK = 128
SWA_QT_ROWS = SWA_QH * SWA_HD
SWA_ONES_ROWS = 16

D_FF = 4096

F32 = jnp.float32
BF16 = jnp.bfloat16

VMEM_LIMIT_BYTES = 56 * 1024 * 1024

PROJ_TILE = 1024
CORE_TILE = 1024
DENSE_TILE = 1024

NT_DIMS = (((1,), (1,)), ((), ()))
TN_DIMS = (((0,), (0,)), ((), ()))


def _rms(x, g):
    ms = jnp.mean(x * x, axis=-1, keepdims=True)
    return x * lax.rsqrt(ms + NORM_EPS) * g


def _layer_spec(stacked, layer):
    nd = stacked.ndim - 1
    return pl.BlockSpec((None,) + stacked.shape[1:], lambda *_: (layer,) + (0,) * nd,
                        pipeline_mode=pl.Buffered(1))


def _params(*sem):
    return pltpu.CompilerParams(dimension_semantics=sem, vmem_limit_bytes=VMEM_LIMIT_BYTES)


def _gla_in_kernel(x_ref, g_ref, w_ref, wlrt_ref, wg3_ref, bg_ref, gn_ref,
                   qd_ref, ki_ref, ke_ref, v_ref, sg_ref, dec_ref, *, chunks):
    c = GLA_CHUNK
    h = _rms(x_ref[...], g_ref[...]).astype(BF16)
    glr_t = lax.dot_general(wlrt_ref[...], h, NT_DIMS, preferred_element_type=F32)
    hi = glr_t.astype(BF16)
    lo = (glr_t - hi.astype(F32)).astype(BF16)
    z = lax.dot_general(jnp.concatenate([hi, lo, hi], axis=0), wg3_ref[...], TN_DIMS,
                        preferred_element_type=F32) + bg_ref[...]
    la = (jnp.minimum(z, 0.0) - jnp.log(1.0 + jnp.exp(-jnp.abs(z)))) * (1.0 / GLA_NORMALIZER)
    la_hi = la.astype(BF16)
    la_lo = (la - la_hi.astype(F32)).astype(BF16)

    row = lax.broadcasted_iota(jnp.int32, (c, 2 * c), 0)
    col = lax.broadcasted_iota(jnp.int32, (c, 2 * c), 1)
    tril2 = jnp.where((col % c) <= row, 1.0, 0.0).astype(BF16)

    half = GLA_DV // 2
    qf = jnp.dot(h, w_ref[:, :GLA_DK], preferred_element_type=F32)
    for p in range(2):
        gate = jnp.dot(h, w_ref[:, 2048 + p * half:2048 + (p + 1) * half], preferred_element_type=F32)
        sg_ref[:, p * half:(p + 1) * half] = (gate * jax.nn.sigmoid(gate)
                                              * gn_ref[:, p * half:(p + 1) * half]).astype(BF16)
    kf = jnp.dot(h, w_ref[:, GLA_DK:2 * GLA_DK], preferred_element_type=F32)
    scale = GLA_HK ** -0.5
    for ci in range(chunks):
        rows = slice(ci * c, (ci + 1) * c)
        bc = jnp.dot(tril2, jnp.concatenate([la_hi[rows], la_lo[rows]], axis=0),
                     preferred_element_type=F32)
        b_last = bc[c - 1:c, :]
        k = kf[rows, :]
        qd_ref[rows, :] = (qf[rows, :] * (scale * jnp.exp(bc))).astype(BF16)
        ki_ref[rows, :] = (k * jnp.exp(-bc)).astype(BF16)
        ke_ref[rows, :] = (k * jnp.exp(b_last - bc)).astype(BF16)
        dec_ref[ci:ci + 1, :] = jnp.exp(b_last)
    for p in range(2):
        v_ref[:, p * half:(p + 1) * half] = jnp.dot(h, w_ref[:, 1024 + p * half:1024 + (p + 1) * half],
                                                    preferred_element_type=F32).astype(BF16)


def _gla_in(x, g, w, wlrt, wg3, bg, gn, layer, j, tm):
    n = x.shape[0]
    chunks = tm // GLA_CHUNK
    tok = lambda c: pl.BlockSpec((tm, c), lambda i: (i, 0))
    return pl.pallas_call(
        functools.partial(_gla_in_kernel, chunks=chunks),
        grid=(n // tm,),
        in_specs=[tok(D_MODEL), _layer_spec(g, layer), _layer_spec(w, j), _layer_spec(wlrt, j),
                  _layer_spec(wg3, j), _layer_spec(bg, j), _layer_spec(gn, j)],
        out_specs=[tok(GLA_DK), tok(GLA_DK), tok(GLA_DK), tok(1024), tok(1024),
                   pl.BlockSpec((chunks, GLA_DK), lambda i: (i, 0))],
        out_shape=[jax.ShapeDtypeStruct((n, GLA_DK), BF16)] * 3 + [jax.ShapeDtypeStruct((n, 1024), BF16)] * 2
                  + [jax.ShapeDtypeStruct((n // GLA_CHUNK, GLA_DK), F32)],
        compiler_params=_params("parallel"),
        name="gla_in",
    )(x, g, w, wlrt, wg3, bg, gn)


def _gla_core_kernel(qd_ref, ki_ref, ke_ref, v_ref, sg_ref, dec_ref, o_ref, st_ref, *, chunks):
    @pl.when(pl.program_id(1) == 0)
    def _():
        st_ref[...] = jnp.zeros_like(st_ref)

    c = GLA_CHUNK
    row = lax.broadcasted_iota(jnp.int32, (c, c), 0)
    col = lax.broadcasted_iota(jnp.int32, (c, c), 1)
    causal = col <= row

    def scores(ci):
        rows = slice(ci * c, (ci + 1) * c)
        return [lax.dot_general(qd_ref[rows, h * GLA_HK:(h + 1) * GLA_HK],
                                ki_ref[rows, h * GLA_HK:(h + 1) * GLA_HK], NT_DIMS,
                                preferred_element_type=F32) for h in range(GLA_HEADS)]

    att_next = scores(0)
    for ci in range(chunks):
        rows = slice(ci * c, (ci + 1) * c)
        att_all = att_next
        if ci + 1 < chunks:
            att_next = scores(ci + 1)
        decay = dec_ref[ci:ci + 1, :]
        for h in range(GLA_HEADS):
            ks = slice(h * GLA_HK, (h + 1) * GLA_HK)
            vs = slice(h * GLA_HV, (h + 1) * GLA_HV)
            v = v_ref[rows, vs]
            att = jnp.where(causal, att_all[h], 0.0).astype(BF16)
            st = st_ref[h]
            o = (jnp.dot(att, v, preferred_element_type=F32)
                 + lax.dot_general(qd_ref[rows, ks], st.astype(BF16), NT_DIMS,
                                   preferred_element_type=F32))
            st_ref[h] = st * decay[:, ks] + lax.dot_general(v, ke_ref[rows, ks], TN_DIMS,
                                                            preferred_element_type=F32)
            ms = jnp.mean(o * o, axis=-1, keepdims=True)
            o_ref[rows, vs] = (o * lax.rsqrt(ms + NORM_EPS) * sg_ref[rows, vs].astype(F32)).astype(BF16)


def _gla_core(qd, ki, ke, v, sg, dec, batch, seq, tc):
    n = batch * seq
    steps = seq // tc
    chunks = tc // GLA_CHUNK
    tok = lambda c: pl.BlockSpec((tc, c), lambda b, i: (b * steps + i, 0))
    return pl.pallas_call(
        functools.partial(_gla_core_kernel, chunks=chunks),
        grid=(batch, steps),
        in_specs=[tok(GLA_DK), tok(GLA_DK), tok(GLA_DK), tok(1024), tok(1024),
                  pl.BlockSpec((chunks, GLA_DK), lambda b, i: (b * steps + i, 0))],
        out_specs=tok(1024),
        out_shape=jax.ShapeDtypeStruct((n, GLA_DV), BF16),
        scratch_shapes=[pltpu.VMEM((GLA_HEADS, GLA_HV, GLA_HK), F32)],
        compiler_params=_params("parallel", "arbitrary"),
        name="gla_core",
    )(qd, ki, ke, v, sg, dec)


def _swa_in_kernel(x_ref, g_ref, wqt_ref, bqt_ref, wk_ref, bk_ref, wvt_ref, bvt_ref,
                   qt_ref, kp_ref, vt_ref):
    h = _rms(x_ref[...], g_ref[...]).astype(BF16)
    kp_ref[...] = (jnp.dot(h, wk_ref[...], preferred_element_type=F32) + bk_ref[...]).astype(BF16)
    vt_ref[...] = (lax.dot_general(wvt_ref[...], h, NT_DIMS, preferred_element_type=F32)
                   + bvt_ref[...]).astype(BF16)
    qt_ref[...] = (lax.dot_general(wqt_ref[...], h, NT_DIMS, preferred_element_type=F32)
                   + bqt_ref[...]).astype(BF16)


def _swa_in(x, g, wqt, bqt, wk, bk, wvt, bvt, layer, j, tm):
    n = x.shape[0]
    consts = (wqt, bqt, wk, bk, wvt, bvt)
    return pl.pallas_call(
        _swa_in_kernel,
        grid=(n // tm,),
        in_specs=[pl.BlockSpec((tm, D_MODEL), lambda i: (i, 0)), _layer_spec(g, layer)]
                 + [_layer_spec(c, j) for c in consts],
        out_specs=[pl.BlockSpec((SWA_QT_ROWS, tm), lambda i: (0, i)),
                   pl.BlockSpec((tm, 128), lambda i: (i, 0)),
                   pl.BlockSpec((128, tm), lambda i: (0, i))],
        out_shape=[jax.ShapeDtypeStruct((SWA_QT_ROWS, n), BF16),
                   jax.ShapeDtypeStruct((n, 128), BF16),
                   jax.ShapeDtypeStruct((128, n), BF16)],
        compiler_params=_params("parallel"),
        name="swa_in",
    )(x, g, *consts)


def _swa_core_kernel(sink_ref, qt_ref, kp_ref, kpp_ref, vt_ref, vtp_ref, ot_ref, *, blocks):
    first_tile = pl.program_id(1) == 0
    blk = SWA_BLOCK
    sj = lax.broadcasted_iota(jnp.int32, (2 * blk, blk), 0)
    qi = lax.broadcasted_iota(jnp.int32, (2 * blk, blk), 1)
    band = (sj > qi) & (sj <= qi + blk)
    ones = jnp.ones((SWA_ONES_ROWS, 2 * blk), BF16)
    klane = lax.broadcasted_iota(jnp.int32, (2 * blk, SWA_KVH * SWA_HD), 1)
    log2e = math.log2(math.e)

    def heads_of(kh):
        return [kh * SWA_GROUP + g for g in range(SWA_GROUP)]

    def scores(j, kh):
        cols = slice(j * blk, (j + 1) * blk)
        k_prev = kpp_ref[...] if j == 0 else kp_ref[(j - 1) * blk:j * blk, :]
        k_band = jnp.concatenate([k_prev, kp_ref[cols, :]], axis=0)
        k_band = jnp.where((klane // SWA_HD) == kh, k_band, jnp.zeros_like(k_band))
        first = lambda hq: (hq - kh) * SWA_HD
        qt = jnp.concatenate([qt_ref[first(hq):first(hq) + 2 * SWA_HD, cols] for hq in heads_of(kh)], axis=1)
        return jnp.dot(k_band, qt, preferred_element_type=F32)

    def finish(j, kh, s):
        cols = slice(j * blk, (j + 1) * blk)
        valid = band & (jnp.logical_not(first_tile) | (sj >= blk)) if j == 0 else band
        es, mxs = [], []
        for g, hq in enumerate(heads_of(kh)):
            sink = sink_ref[hq] * log2e
            sg = jnp.where(valid, s[:, g * blk:(g + 1) * blk], -jnp.inf)
            mx = jnp.maximum(jnp.max(sg, axis=0, keepdims=True), sink)
            es.append(jnp.exp2(sg - mx).astype(BF16))
            mxs.append(jnp.exp2(sink - mx))
        e = jnp.concatenate(es, axis=1)
        rows = slice(kh * SWA_HD, (kh + 1) * SWA_HD)
        vt_prev = vtp_ref[rows, :] if j == 0 else vt_ref[rows, (j - 1) * blk:j * blk]
        lhs = jnp.concatenate([jnp.concatenate([vt_prev, vt_ref[rows, cols]], axis=1), ones], axis=0)
        o_aug = jnp.dot(lhs, e, preferred_element_type=F32)
        den = o_aug[SWA_HD:SWA_HD + 1, :] + jnp.concatenate(mxs, axis=1)
        o = (o_aug[:SWA_HD, :] / den).astype(BF16)
        for g, hq in enumerate(heads_of(kh)):
            ot_ref[hq * SWA_HD:(hq + 1) * SWA_HD, cols] = o[:, g * blk:(g + 1) * blk]

    units = [(j, kh) for j in range(blocks) for kh in range(SWA_KVH)]
    s_next = scores(*units[0])
    for idx, unit in enumerate(units):
        s = s_next
        if idx + 1 < len(units):
            s_next = scores(*units[idx + 1])
        finish(*unit, s)


def _swa_core(sinks, qt, kp, vt, batch, seq, tq):
    n = batch * seq
    steps = seq // tq
    bpt = tq // SWA_BLOCK
    tile = lambda b, i: b * steps + i
    prev_blk = lambda b, i: jnp.maximum(tile(b, i) * bpt - 1, 0)
    return pl.pallas_call(
        functools.partial(_swa_core_kernel, blocks=bpt),
        grid=(batch, steps),
        in_specs=[pl.BlockSpec(memory_space=pltpu.SMEM),
                  pl.BlockSpec((SWA_QT_ROWS, tq), lambda b, i: (0, tile(b, i))),
                  pl.BlockSpec((tq, 128), lambda b, i: (tile(b, i), 0)),
                  pl.BlockSpec((SWA_BLOCK, 128), lambda b, i: (prev_blk(b, i), 0)),
                  pl.BlockSpec((128, tq), lambda b, i: (0, tile(b, i))),
                  pl.BlockSpec((128, SWA_BLOCK), lambda b, i: (0, prev_blk(b, i)))],
        out_specs=pl.BlockSpec((SWA_QH * SWA_HD, tq), lambda b, i: (0, tile(b, i))),
        out_shape=jax.ShapeDtypeStruct((SWA_QH * SWA_HD, n), BF16),
        compiler_params=_params("parallel", "arbitrary"),
        name="swa_core",
    )(sinks, qt, kp, kp, vt, vt)


MLP_FF_CHUNK = 512


def _dense_kernel(x_ref, o_ref, wo_ref, bo_ref, gpost_ref, g1_ref, g2_ref, wu_ref, wd_ref,
                  y_ref, a_ref, *, o_transposed):
    dims = TN_DIMS if o_transposed else (((1,), (0,)), ((), ()))
    tm = x_ref.shape[0]
    halves = [slice(0, tm // 2), slice(tm // 2, tm)]
    xs, hs = [], []
    for r in halves:
        o = o_ref[:, r] if o_transposed else o_ref[r, :]
        m = lax.dot_general(o, wo_ref[...], dims, preferred_element_type=F32) + bo_ref[...]
        x = x_ref[r, :] + _rms(m, gpost_ref[...])
        xs.append(x)
        hs.append(_rms(x, g1_ref[...]).astype(BF16))
    h = jnp.concatenate(hs, axis=0)
    for j in range(D_FF // MLP_FF_CHUNK):
        cols = slice(j * MLP_FF_CHUNK, (j + 1) * MLP_FF_CHUNK)
        u = jnp.maximum(jnp.dot(h, wu_ref[:, cols], preferred_element_type=F32), 0.0)
        a_ref[:, cols] = (u * u).astype(BF16)
    for r, x in zip(halves, xs):
        f = jnp.dot(a_ref[r, :], wd_ref[...], preferred_element_type=F32)
        y_ref[r, :] = x + _rms(f, g2_ref[...])


def _dense(x, o, wo, bo, gpost, g1, g2, wu, wd, layer, j, tm, o_transposed):
    n = x.shape[0]
    tok = pl.BlockSpec((tm, D_MODEL), lambda i: (i, 0))
    o_spec = pl.BlockSpec((1024, tm), lambda i: (0, i)) if o_transposed else tok
    return pl.pallas_call(
        functools.partial(_dense_kernel, o_transposed=o_transposed),
        grid=(n // tm,),
        in_specs=[tok, o_spec, _layer_spec(wo, j), _layer_spec(bo, j), _layer_spec(gpost, layer),
                  _layer_spec(g1, layer), _layer_spec(g2, layer), _layer_spec(wu, layer), _layer_spec(wd, layer)],
        out_specs=tok,
        out_shape=jax.ShapeDtypeStruct((n, D_MODEL), F32),
        scratch_shapes=[pltpu.VMEM((tm, D_FF), BF16)],
        compiler_params=_params("parallel"),
        name="dense",
    )(x, o, wo, bo, gpost, g1, g2, wu, wd)


def _split_bf16(w):
    hi = w.astype(BF16)
    return hi, (w - hi.astype(F32)).astype(BF16)


def _gla_weights(w_in, w_gate_up, b_gate_up, g_norm):
    layers = w_in.shape[0]
    w_main = w_in[:, :, :3072].astype(BF16)
    w_lr_t = jnp.swapaxes(w_in[:, :, 3072:], 1, 2).astype(BF16)
    wg_hi, wg_lo = _split_bf16(w_gate_up)
    gn = jnp.tile(g_norm, (1, GLA_HEADS)).reshape(layers, 1, GLA_DV)
    return (w_main, w_lr_t, jnp.concatenate([wg_hi, wg_hi, wg_lo], axis=1),
            b_gate_up.reshape(layers, 1, GLA_DK), gn)


def _swa_weights(w_in, b_in):
    layers = w_in.shape[0]
    nq = SWA_QH * SWA_HD
    qscale = SWA_HD ** -0.5 * math.log2(math.e)
    wqt = jnp.swapaxes(w_in[:, :, :nq] * qscale, 1, 2).astype(BF16)
    bqt = (b_in[:, :nq] * qscale).reshape(layers, nq, 1)
    wk = w_in[:, :, nq:nq + 128].astype(BF16)
    bk = b_in[:, nq:nq + 128].reshape(layers, 1, 128)
    wvt = jnp.swapaxes(w_in[:, :, nq + 128:], 1, 2).astype(BF16)
    bvt = b_in[:, nq + 128:].reshape(layers, 128, 1)
    return wqt, bqt, wk, bk, wvt, bvt


def kernel(x, ln_mix_pre, ln_mix_post, ln_mlp_pre, ln_mlp_post, gla_w_in, gla_w_gate_up, gla_b_gate_up, gla_g_norm, gla_w_out, swa_w_in, swa_b_in, swa_sinks, swa_w_out, swa_b_out, mlp_w_up, mlp_w_down):
    batch, seq, d = x.shape
    n = batch * seq
    depth = ln_mix_pre.shape[0]
    tp, tc, tm = min(PROJ_TILE, n), min(CORE_TILE, seq), min(DENSE_TILE, n)
    xf = x.reshape(n, d)
    gain = lambda t: t.reshape(t.shape[0], 1, t.shape[1])
    g_pre, g_post, g1, g2 = gain(ln_mix_pre), gain(ln_mix_post), gain(ln_mlp_pre), gain(ln_mlp_post)
    gla_w = _gla_weights(gla_w_in, gla_w_gate_up, gla_b_gate_up, gla_g_norm)
    swa_w = _swa_weights(swa_w_in, swa_b_in)
    gla_wo, swa_wo = gla_w_out.astype(BF16), swa_w_out.astype(BF16)
    gla_bo = jnp.zeros((gla_w_out.shape[0], 1, D_MODEL), F32)
    swa_bo = gain(swa_b_out)
    wu, wd = mlp_w_up.astype(BF16), mlp_w_down.astype(BF16)
    for i in range(depth):
        j = i // 2
        if i % 2 == 0:
            qd, ki, ke, v, sg, dec = _gla_in(xf, g_pre, *gla_w, i, j, tp)
            o = _gla_core(qd, ki, ke, v, sg, dec, batch, seq, tc)
            wo, bo = gla_wo, gla_bo
        else:
            qt, kp, vt = _swa_in(xf, g_pre, *swa_w, i, j, tp)
            o = _swa_core(swa_sinks[j], qt, kp, vt, batch, seq, tc)
            wo, bo = swa_wo, swa_bo
        xf = _dense(xf, o, wo, bo, g_post, g1, g2, wu, wd, i, j, tm, o_transposed=(i % 2 == 1))
    return xf.reshape(batch, seq, d)
```

```python
import functools
import math

import jax
import jax.numpy as jnp
from jax import lax
from jax.experimental import pallas as pl
from jax.experimental.pallas import tpu as pltpu

D_MODEL = 1024
NORM_EPS = 1e-6

GLA_HEADS = 4
GLA_DK = 512
GLA_DV = 1024
GLA_HK = 128
GLA_HV = 256
GLA_RANK = 16
GLA_RANK_PAD = 128
GLA_NORMALIZER = 16.0
GLA_CHUNK = 64

SWA_HD = 64
SWA_QH = 16
SWA_GROUP = 8
SWA_KVH = 2
SWA_BLOCK = 128
SWA_QT_ROWS = SWA_QH * SWA_HD
SWA_ONES_ROWS = 16

D_FF = 4096

F32 = jnp.float32
BF16 = jnp.bfloat16

VMEM_LIMIT_BYTES = 56 * 1024 * 1024

PROJ_TILE = 1024
GLA_CORE_TILE = 1024
SWA_CORE_TILE = 512
DENSE_TILE = 1024

NT_DIMS = (((1,), (1,)), ((), ()))
TN_DIMS = (((0,), (0,)), ((), ()))


def _rms(x, g):
    ms = jnp.mean(x * x, axis=-1, keepdims=True)
    return x * lax.rsqrt(ms + NORM_EPS) * g


def _layer_spec(stacked, layer):
    nd = stacked.ndim - 1
    return pl.BlockSpec((None,) + stacked.shape[1:], lambda *_: (layer,) + (0,) * nd,
                        pipeline_mode=pl.Buffered(1))


def _params(*sem):
    return pltpu.CompilerParams(dimension_semantics=sem, vmem_limit_bytes=VMEM_LIMIT_BYTES)


def _gla_in_kernel(x_ref, g_ref, w_ref, wlrt_ref, wg3_ref, bg_ref, gn_ref,
                   qd_ref, ki_ref, ke_ref, v_ref, sg_ref, dec_ref, *, chunks):
    c = GLA_CHUNK
    h = _rms(x_ref[...], g_ref[...]).astype(BF16)
    glr_t = lax.dot_general(wlrt_ref[...], h, NT_DIMS, preferred_element_type=F32)
    hi = glr_t.astype(BF16)
    lo = (glr_t - hi.astype(F32)).astype(BF16)
    z = lax.dot_general(jnp.concatenate([hi, lo, hi], axis=0), wg3_ref[...], TN_DIMS,
                        preferred_element_type=F32) + bg_ref[...]
    la = (jnp.minimum(z, 0.0) - jnp.log(1.0 + jnp.exp(-jnp.abs(z)))) * (1.0 / GLA_NORMALIZER)
    la_hi = la.astype(BF16)
    la_lo = (la - la_hi.astype(F32)).astype(BF16)

    row = lax.broadcasted_iota(jnp.int32, (c, 2 * c), 0)
    col = lax.broadcasted_iota(jnp.int32, (c, 2 * c), 1)
    tril2 = jnp.where((col % c) <= row, 1.0, 0.0).astype(BF16)

    half = GLA_DV // 2
    qf = jnp.dot(h, w_ref[:, :GLA_DK], preferred_element_type=F32)
    for p in range(2):
        gate = jnp.dot(h, w_ref[:, 2048 + p * half:2048 + (p + 1) * half], preferred_element_type=F32)
        sg_ref[:, p * half:(p + 1) * half] = (gate * jax.nn.sigmoid(gate)
                                              * gn_ref[:, p * half:(p + 1) * half]).astype(BF16)
    kf = jnp.dot(h, w_ref[:, GLA_DK:2 * GLA_DK], preferred_element_type=F32)
    scale = GLA_HK ** -0.5
    for ci in range(chunks):
        rows = slice(ci * c, (ci + 1) * c)
        bc = jnp.dot(tril2, jnp.concatenate([la_hi[rows], la_lo[rows]], axis=0),
                     preferred_element_type=F32)
        b_last = bc[c - 1:c, :]
        k = kf[rows, :]
        qd_ref[rows, :] = (qf[rows, :] * (scale * jnp.exp(bc))).astype(BF16)
        ki_ref[rows, :] = (k * jnp.exp(-bc)).astype(BF16)
        ke_ref[rows, :] = (k * jnp.exp(b_last - bc)).astype(BF16)
        dec_ref[ci:ci + 1, :] = jnp.exp(b_last)
    for p in range(2):
        v_ref[:, p * half:(p + 1) * half] = jnp.dot(h, w_ref[:, 1024 + p * half:1024 + (p + 1) * half],
                                                    preferred_element_type=F32).astype(BF16)


def _gla_in(x, g, w, wlrt, wg3, bg, gn, layer, j, tm):
    n = x.shape[0]
    chunks = tm // GLA_CHUNK
    tok = lambda c: pl.BlockSpec((tm, c), lambda i: (i, 0))
    return pl.pallas_call(
        functools.partial(_gla_in_kernel, chunks=chunks),
        grid=(n // tm,),
        in_specs=[tok(D_MODEL), _layer_spec(g, layer), _layer_spec(w, j), _layer_spec(wlrt, j),
                  _layer_spec(wg3, j), _layer_spec(bg, j), _layer_spec(gn, j)],
        out_specs=[tok(GLA_DK), tok(GLA_DK), tok(GLA_DK), tok(1024), tok(1024),
                   pl.BlockSpec((chunks, GLA_DK), lambda i: (i, 0))],
        out_shape=[jax.ShapeDtypeStruct((n, GLA_DK), BF16)] * 3 + [jax.ShapeDtypeStruct((n, 1024), BF16)] * 2
                  + [jax.ShapeDtypeStruct((n // GLA_CHUNK, GLA_DK), F32)],
        compiler_params=_params("parallel"),
        name="gla_in",
    )(x, g, w, wlrt, wg3, bg, gn)


def _gla_core_kernel(qd_ref, ki_ref, ke_ref, v_ref, sg_ref, dec_ref, o_ref, st_ref, *, chunks):
    @pl.when(pl.program_id(1) == 0)
    def _():
        st_ref[...] = jnp.zeros_like(st_ref)

    c = GLA_CHUNK
    row = lax.broadcasted_iota(jnp.int32, (c, c), 0)
    col = lax.broadcasted_iota(jnp.int32, (c, c), 1)
    causal = col <= row

    def scores(ci):
        rows = slice(ci * c, (ci + 1) * c)
        return [lax.dot_general(qd_ref[rows, h * GLA_HK:(h + 1) * GLA_HK],
                                ki_ref[rows, h * GLA_HK:(h + 1) * GLA_HK], NT_DIMS,
                                preferred_element_type=F32) for h in range(GLA_HEADS)]

    att_next = scores(0)
    for ci in range(chunks):
        rows = slice(ci * c, (ci + 1) * c)
        att_all = att_next
        if ci + 1 < chunks:
            att_next = scores(ci + 1)
        decay = dec_ref[ci:ci + 1, :]
        for h in range(GLA_HEADS):
            ks = slice(h * GLA_HK, (h + 1) * GLA_HK)
            vs = slice(h * GLA_HV, (h + 1) * GLA_HV)
            v = v_ref[rows, vs]
            att = jnp.where(causal, att_all[h], 0.0).astype(BF16)
            st = st_ref[h]
            o = (jnp.dot(att, v, preferred_element_type=F32)
                 + lax.dot_general(qd_ref[rows, ks], st.astype(BF16), NT_DIMS,
                                   preferred_element_type=F32))
            st_ref[h] = st * decay[:, ks] + lax.dot_general(v, ke_ref[rows, ks], TN_DIMS,
                                                            preferred_element_type=F32)
            ms = jnp.mean(o * o, axis=-1, keepdims=True)
            o_ref[rows, vs] = (o * lax.rsqrt(ms + NORM_EPS) * sg_ref[rows, vs].astype(F32)).astype(BF16)


def _gla_core(qd, ki, ke, v, sg, dec, batch, seq, tc):
    n = batch * seq
    steps = seq // tc
    chunks = tc // GLA_CHUNK
    tok = lambda c: pl.BlockSpec((tc, c), lambda b, i: (b * steps + i, 0))
    return pl.pallas_call(
        functools.partial(_gla_core_kernel, chunks=chunks),
        grid=(batch, steps),
        in_specs=[tok(GLA_DK), tok(GLA_DK), tok(GLA_DK), tok(1024), tok(1024),
                  pl.BlockSpec((chunks, GLA_DK), lambda b, i: (b * steps + i, 0))],
        out_specs=tok(1024),
        out_shape=jax.ShapeDtypeStruct((n, GLA_DV), BF16),
        scratch_shapes=[pltpu.VMEM((GLA_HEADS, GLA_HV, GLA_HK), F32)],
        compiler_params=_params("parallel", "arbitrary"),
        name="gla_core",
    )(qd, ki, ke, v, sg, dec)


def _swa_in_kernel(x_ref, g_ref, wqt_ref, bqt_ref, wk_ref, bk_ref, wvt_ref, bvt_ref,
                   qt_ref, kp_ref, vt_ref):
    h = _rms(x_ref[...], g_ref[...]).astype(BF16)
    kp_ref[...] = (jnp.dot(h, wk_ref[...], preferred_element_type=F32) + bk_ref[...]).astype(BF16)
    vt_ref[...] = (lax.dot_general(wvt_ref[...], h, NT_DIMS, preferred_element_type=F32)
                   + bvt_ref[...]).astype(BF16)
    qt_ref[...] = (lax.dot_general(wqt_ref[...], h, NT_DIMS, preferred_element_type=F32)
                   + bqt_ref[...]).astype(BF16)


def _swa_in(x, g, wqt, bqt, wk, bk, wvt, bvt, layer, j, tm):
    n = x.shape[0]
    consts = (wqt, bqt, wk, bk, wvt, bvt)
    return pl.pallas_call(
        _swa_in_kernel,
        grid=(n // tm,),
        in_specs=[pl.BlockSpec((tm, D_MODEL), lambda i: (i, 0)), _layer_spec(g, layer)]
                 + [_layer_spec(c, j) for c in consts],
        out_specs=[pl.BlockSpec((SWA_QT_ROWS, tm), lambda i: (0, i)),
                   pl.BlockSpec((tm, 128), lambda i: (i, 0)),
                   pl.BlockSpec((128, tm), lambda i: (0, i))],
        out_shape=[jax.ShapeDtypeStruct((SWA_QT_ROWS, n), BF16),
                   jax.ShapeDtypeStruct((n, 128), BF16),
                   jax.ShapeDtypeStruct((128, n), BF16)],
        compiler_params=_params("parallel"),
        name="swa_in",
    )(x, g, *consts)


def _swa_core_kernel(sink_ref, qt_ref, kp_ref, kpp_ref, vt_ref, vtp_ref, ot_ref, *, blocks):
    first_tile = pl.program_id(1) == 0
    blk = SWA_BLOCK
    sj = lax.broadcasted_iota(jnp.int32, (2 * blk, blk), 0)
    qi = lax.broadcasted_iota(jnp.int32, (2 * blk, blk), 1)
    band = (sj > qi) & (sj <= qi + blk)
    ones = jnp.ones((SWA_ONES_ROWS, 2 * blk), BF16)
    klane = lax.broadcasted_iota(jnp.int32, (2 * blk, SWA_KVH * SWA_HD), 1)
    log2e = math.log2(math.e)

    def heads_of(kh):
        return [kh * SWA_GROUP + g for g in range(SWA_GROUP)]

    def scores(j, kh):
        cols = slice(j * blk, (j + 1) * blk)
        k_prev = kpp_ref[...] if j == 0 else kp_ref[(j - 1) * blk:j * blk, :]
        k_band = jnp.concatenate([k_prev, kp_ref[cols, :]], axis=0)
        k_band = jnp.where((klane // SWA_HD) == kh, k_band, jnp.zeros_like(k_band))
        first = lambda hq: (hq - kh) * SWA_HD
        qt = jnp.concatenate([qt_ref[first(hq):first(hq) + 2 * SWA_HD, cols] for hq in heads_of(kh)], axis=1)
        return jnp.dot(k_band, qt, preferred_element_type=F32)

    def finish(j, kh, s):
        cols = slice(j * blk, (j + 1) * blk)
        valid = band & (jnp.logical_not(first_tile) | (sj >= blk)) if j == 0 else band
        es, mxs = [], []
        for g, hq in enumerate(heads_of(kh)):
            sink = sink_ref[hq] * log2e
            sg = jnp.where(valid, s[:, g * blk:(g + 1) * blk], -jnp.inf)
            mx = jnp.maximum(jnp.max(sg, axis=0, keepdims=True), sink)
            es.append(jnp.exp2(sg - mx).astype(BF16))
            mxs.append(jnp.exp2(sink - mx))
        e = jnp.concatenate(es, axis=1)
        rows = slice(kh * SWA_HD, (kh + 1) * SWA_HD)
        vt_prev = vtp_ref[rows, :] if j == 0 else vt_ref[rows, (j - 1) * blk:j * blk]
        lhs = jnp.concatenate([jnp.concatenate([vt_prev, vt_ref[rows, cols]], axis=1), ones], axis=0)
        o_aug = jnp.dot(lhs, e, preferred_element_type=F32)
        den = o_aug[SWA_HD:SWA_HD + 1, :] + jnp.concatenate(mxs, axis=1)
        o = (o_aug[:SWA_HD, :] / den).astype(BF16)
        for g, hq in enumerate(heads_of(kh)):
            ot_ref[hq * SWA_HD:(hq + 1) * SWA_HD, cols] = o[:, g * blk:(g + 1) * blk]

    units = [(j, kh) for j in range(blocks) for kh in range(SWA_KVH)]
    s_next = scores(*units[0])
    for idx, unit in enumerate(units):
        s = s_next
        if idx + 1 < len(units):
            s_next = scores(*units[idx + 1])
        finish(*unit, s)


def _swa_core(sinks, qt, kp, vt, batch, seq, tq):
    n = batch * seq
    steps = seq // tq
    bpt = tq // SWA_BLOCK
    tile = lambda b, i: b * steps + i
    prev_blk = lambda b, i: jnp.maximum(tile(b, i) * bpt - 1, 0)
    return pl.pallas_call(
        functools.partial(_swa_core_kernel, blocks=bpt),
        grid=(batch, steps),
        in_specs=[pl.BlockSpec(memory_space=pltpu.SMEM),
                  pl.BlockSpec((SWA_QT_ROWS, tq), lambda b, i: (0, tile(b, i))),
                  pl.BlockSpec((tq, 128), lambda b, i: (tile(b, i), 0)),
                  pl.BlockSpec((SWA_BLOCK, 128), lambda b, i: (prev_blk(b, i), 0)),
                  pl.BlockSpec((128, tq), lambda b, i: (0, tile(b, i))),
                  pl.BlockSpec((128, SWA_BLOCK), lambda b, i: (0, prev_blk(b, i)))],
        out_specs=pl.BlockSpec((SWA_QH * SWA_HD, tq), lambda b, i: (0, tile(b, i))),
        out_shape=jax.ShapeDtypeStruct((SWA_QH * SWA_HD, n), BF16),
        compiler_params=_params("parallel", "arbitrary"),
        name="swa_core",
    )(sinks, qt, kp, kp, vt, vt)


MLP_FF_CHUNK = 512


def _dense_kernel(x_ref, o_ref, wo_ref, bo_ref, gpost_ref, g1_ref, g2_ref, wu_ref, wd_ref,
                  y_ref, a_ref, *, o_transposed):
    dims = TN_DIMS if o_transposed else (((1,), (0,)), ((), ()))
    tm = x_ref.shape[0]
    halves = [slice(0, tm // 2), slice(tm // 2, tm)]
    xs, hs = [], []
    for r in halves:
        o = o_ref[:, r] if o_transposed else o_ref[r, :]
        m = lax.dot_general(o, wo_ref[...], dims, preferred_element_type=F32) + bo_ref[...]
        x = x_ref[r, :] + _rms(m, gpost_ref[...])
        xs.append(x)
        hs.append(_rms(x, g1_ref[...]).astype(BF16))
    for r, h in zip(halves, hs):
        for j in range(D_FF // MLP_FF_CHUNK):
            cols = slice(j * MLP_FF_CHUNK, (j + 1) * MLP_FF_CHUNK)
            u = jnp.maximum(jnp.dot(h, wu_ref[:, cols], preferred_element_type=F32), 0.0)
            a_ref[r, cols] = (u * u).astype(BF16)
    for r, x in zip(halves, xs):
        f = jnp.dot(a_ref[r, :], wd_ref[...], preferred_element_type=F32)
        y_ref[r, :] = x + _rms(f, g2_ref[...])


def _dense(x, o, wo, bo, gpost, g1, g2, wu, wd, layer, j, tm, o_transposed):
    n = x.shape[0]
    tok = pl.BlockSpec((tm, D_MODEL), lambda i: (i, 0))
    o_spec = pl.BlockSpec((1024, tm), lambda i: (0, i)) if o_transposed else tok
    return pl.pallas_call(
        functools.partial(_dense_kernel, o_transposed=o_transposed),
        grid=(n // tm,),
        in_specs=[tok, o_spec, _layer_spec(wo, j), _layer_spec(bo, j), _layer_spec(gpost, layer),
                  _layer_spec(g1, layer), _layer_spec(g2, layer), _layer_spec(wu, layer), _layer_spec(wd, layer)],
        out_specs=tok,
        out_shape=jax.ShapeDtypeStruct((n, D_MODEL), F32),
        scratch_shapes=[pltpu.VMEM((tm, D_FF), BF16)],
        compiler_params=_params("parallel"),
        name="dense",
    )(x, o, wo, bo, gpost, g1, g2, wu, wd)


def _split_bf16(w):
    hi = w.astype(BF16)
    return hi, (w - hi.astype(F32)).astype(BF16)


def _gla_weights(w_in, w_gate_up, b_gate_up, g_norm):
    layers = w_in.shape[0]
    w_main = w_in[:, :, :3072].astype(BF16)
    w_lr_t = jnp.swapaxes(w_in[:, :, 3072:], 1, 2).astype(BF16)
    wg_hi, wg_lo = _split_bf16(w_gate_up)
    gn = jnp.tile(g_norm, (1, GLA_HEADS)).reshape(layers, 1, GLA_DV)
    return (w_main, w_lr_t, jnp.concatenate([wg_hi, wg_hi, wg_lo], axis=1),
            b_gate_up.reshape(layers, 1, GLA_DK), gn)


def _swa_weights(w_in, b_in):
    layers = w_in.shape[0]
    nq = SWA_QH * SWA_HD
    qscale = SWA_HD ** -0.5 * math.log2(math.e)
    wqt = jnp.swapaxes(w_in[:, :, :nq] * qscale, 1, 2).astype(BF16)
    bqt = (b_in[:, :nq] * qscale).reshape(layers, nq, 1)
    wk = w_in[:, :, nq:nq + 128].astype(BF16)
    bk = b_in[:, nq:nq + 128].reshape(layers, 1, 128)
    wvt = jnp.swapaxes(w_in[:, :, nq + 128:], 1, 2).astype(BF16)
    bvt = b_in[:, nq + 128:].reshape(layers, 128, 1)
    return wqt, bqt, wk, bk, wvt, bvt


def kernel(x, ln_mix_pre, ln_mix_post, ln_mlp_pre, ln_mlp_post, gla_w_in, gla_w_gate_up, gla_b_gate_up, gla_g_norm, gla_w_out, swa_w_in, swa_b_in, swa_sinks, swa_w_out, swa_b_out, mlp_w_up, mlp_w_down):
    batch, seq, d = x.shape
    n = batch * seq
    depth = ln_mix_pre.shape[0]
    tp, tm = min(PROJ_TILE, n), min(DENSE_TILE, n)
    tc, tq = min(GLA_CORE_TILE, seq), min(SWA_CORE_TILE, seq)
    xf = x.reshape(n, d)
    gain = lambda t: t.reshape(t.shape[0], 1, t.shape[1])
    g_pre, g_post, g1, g2 = gain(ln_mix_pre), gain(ln_mix_post), gain(ln_mlp_pre), gain(ln_mlp_post)
    gla_w = _gla_weights(gla_w_in, gla_w_gate_up, gla_b_gate_up, gla_g_norm)
    swa_w = _swa_weights(swa_w_in, swa_b_in)
    gla_wo, swa_wo = gla_w_out.astype(BF16), swa_w_out.astype(BF16)
    gla_bo = jnp.zeros((gla_w_out.shape[0], 1, D_MODEL), F32)
    swa_bo = gain(swa_b_out)
    wu, wd = mlp_w_up.astype(BF16), mlp_w_down.astype(BF16)
    for i in range(depth):
        j = i // 2
        if i % 2 == 0:
            qd, ki, ke, v, sg, dec = _gla_in(xf, g_pre, *gla_w, i, j, tp)
            o = _gla_core(qd, ki, ke, v, sg, dec, batch, seq, tc)
            wo, bo = gla_wo, gla_bo
        else:
            qt, kp, vt = _swa_in(xf, g_pre, *swa_w, i, j, tp)
            o = _swa_core(swa_sinks[j], qt, kp, vt, batch, seq, tq)
            wo, bo = swa_wo, swa_bo
        xf = _dense(xf, o, wo, bo, g_post, g1, g2, wu, wd, i, j, tm, o_transposed=(i % 2 == 1))
    return xf.reshape(batch, seq, d)
```

```python
import functools
import math

import jax
import jax.numpy as jnp
from jax import lax
from jax.experimental import pallas as pl
from jax.experimental.pallas import tpu as pltpu

D_MODEL = 1024
NORM_EPS = 1e-6

GLA_HEADS = 4
GLA_DK = 512
GLA_DV = 1024
GLA_HK = 128
GLA_HV = 256
GLA_RANK = 16
GLA_RANK_PAD = 128
GLA_NORMALIZER = 16.0
GLA_CHUNK = 64

SWA_HD = 64
SWA_QH = 16
SWA_GROUP = 8
SWA_KVH = 2
SWA_BLOCK = 128
SWA_QT_ROWS = SWA_QH * SWA_HD
SWA_ONES_ROWS = 16

D_FF = 4096

F32 = jnp.float32
BF16 = jnp.bfloat16

VMEM_LIMIT_BYTES = 56 * 1024 * 1024

PROJ_TILE = 1024
GLA_CORE_TILE = 1024
SWA_CORE_TILE = 512
DENSE_TILE = 1024

NT_DIMS = (((1,), (1,)), ((), ()))
TN_DIMS = (((0,), (0,)), ((), ()))


def _rms(x, g):
    ms = jnp.mean(x * x, axis=-1, keepdims=True)
    return x * lax.rsqrt(ms + NORM_EPS) * g


def _layer_spec(stacked, layer):
    nd = stacked.ndim - 1
    return pl.BlockSpec((None,) + stacked.shape[1:], lambda *_: (layer,) + (0,) * nd,
                        pipeline_mode=pl.Buffered(1))


def _params(*sem):
    return pltpu.CompilerParams(dimension_semantics=sem, vmem_limit_bytes=VMEM_LIMIT_BYTES)


def _gla_proj_steps(x_ref, g_ref, w_ref, wlrt_ref, wg3_ref, bg_ref, gn_ref, dst, chunks):
    qd_ref, ki_ref, ke_ref, v_ref, sg_ref, dec_ref = dst
    c = GLA_CHUNK
    h = _rms(x_ref[...], g_ref[...]).astype(BF16)
    glr_t = lax.dot_general(wlrt_ref[...], h, NT_DIMS, preferred_element_type=F32)
    hi = glr_t.astype(BF16)
    lo = (glr_t - hi.astype(F32)).astype(BF16)
    z = lax.dot_general(jnp.concatenate([hi, lo, hi], axis=0), wg3_ref[...], TN_DIMS,
                        preferred_element_type=F32) + bg_ref[...]
    la = (jnp.minimum(z, 0.0) - jnp.log(1.0 + jnp.exp(-jnp.abs(z)))) * (1.0 / GLA_NORMALIZER)
    la_hi = la.astype(BF16)
    la_lo = (la - la_hi.astype(F32)).astype(BF16)

    row = lax.broadcasted_iota(jnp.int32, (c, 2 * c), 0)
    col = lax.broadcasted_iota(jnp.int32, (c, 2 * c), 1)
    tril2 = jnp.where((col % c) <= row, 1.0, 0.0).astype(BF16)

    yield
    half = GLA_DV // 2
    qf = jnp.dot(h, w_ref[:, :GLA_DK], preferred_element_type=F32)
    yield
    for p in range(2):
        gate = jnp.dot(h, w_ref[:, 2048 + p * half:2048 + (p + 1) * half], preferred_element_type=F32)
        sg_ref[:, p * half:(p + 1) * half] = (gate * jax.nn.sigmoid(gate)
                                              * gn_ref[:, p * half:(p + 1) * half]).astype(BF16)
        yield
    kf = jnp.dot(h, w_ref[:, GLA_DK:2 * GLA_DK], preferred_element_type=F32)
    yield
    scale = GLA_HK ** -0.5
    bcs = [jnp.dot(tril2, jnp.concatenate([la_hi[ci * c:(ci + 1) * c], la_lo[ci * c:(ci + 1) * c]], axis=0),
                   preferred_element_type=F32) for ci in range(chunks)]
    yield
    for p in range(2):
        v_ref[:, p * half:(p + 1) * half] = jnp.dot(h, w_ref[:, 1024 + p * half:1024 + (p + 1) * half],
                                                    preferred_element_type=F32).astype(BF16)
        for ci in range(p * chunks // 2, (p + 1) * chunks // 2):
            rows = slice(ci * c, (ci + 1) * c)
            bc = bcs[ci]
            b_last = bc[c - 1:c, :]
            k = kf[rows, :]
            qd_ref[rows, :] = (qf[rows, :] * (scale * jnp.exp(bc))).astype(BF16)
            ki_ref[rows, :] = (k * jnp.exp(-bc)).astype(BF16)
            ke_ref[rows, :] = (k * jnp.exp(b_last - bc)).astype(BF16)
            dec_ref[ci:ci + 1, :] = jnp.exp(b_last)
        yield


def _gla_core_steps(src, o_ref, st_ref, chunks):
    qd_ref, ki_ref, ke_ref, v_ref, sg_ref, dec_ref = src
    c = GLA_CHUNK
    row = lax.broadcasted_iota(jnp.int32, (c, c), 0)
    col = lax.broadcasted_iota(jnp.int32, (c, c), 1)
    causal = col <= row

    def scores(ci):
        rows = slice(ci * c, (ci + 1) * c)
        return [lax.dot_general(qd_ref[rows, h * GLA_HK:(h + 1) * GLA_HK],
                                ki_ref[rows, h * GLA_HK:(h + 1) * GLA_HK], NT_DIMS,
                                preferred_element_type=F32) for h in range(GLA_HEADS)]

    att_next = scores(0)
    for ci in range(chunks):
        rows = slice(ci * c, (ci + 1) * c)
        att_all = att_next
        if ci + 1 < chunks:
            att_next = scores(ci + 1)
        decay = dec_ref[ci:ci + 1, :]
        for h in range(GLA_HEADS):
            ks = slice(h * GLA_HK, (h + 1) * GLA_HK)
            vs = slice(h * GLA_HV, (h + 1) * GLA_HV)
            v = v_ref[rows, vs]
            att = jnp.where(causal, att_all[h], 0.0).astype(BF16)
            st = st_ref[h]
            o = (jnp.dot(att, v, preferred_element_type=F32)
                 + lax.dot_general(qd_ref[rows, ks], st.astype(BF16), NT_DIMS,
                                   preferred_element_type=F32))
            st_ref[h] = st * decay[:, ks] + lax.dot_general(v, ke_ref[rows, ks], TN_DIMS,
                                                            preferred_element_type=F32)
            ms = jnp.mean(o * o, axis=-1, keepdims=True)
            o_ref[rows, vs] = (o * lax.rsqrt(ms + NORM_EPS) * sg_ref[rows, vs].astype(F32)).astype(BF16)
        yield


def _gla_in_kernel(x_ref, g_ref, w_ref, wlrt_ref, wg3_ref, bg_ref, gn_ref, *dst, chunks):
    for _ in _gla_proj_steps(x_ref, g_ref, w_ref, wlrt_ref, wg3_ref, bg_ref, gn_ref, dst, chunks):
        pass


def _gla_in(x, g, w, wlrt, wg3, bg, gn, layer, j, tm):
    n = x.shape[0]
    chunks = tm // GLA_CHUNK
    tok = lambda c: pl.BlockSpec((tm, c), lambda i: (i, 0))
    return pl.pallas_call(
        functools.partial(_gla_in_kernel, chunks=chunks),
        grid=(n // tm,),
        in_specs=[tok(D_MODEL), _layer_spec(g, layer), _layer_spec(w, j), _layer_spec(wlrt, j),
                  _layer_spec(wg3, j), _layer_spec(bg, j), _layer_spec(gn, j)],
        out_specs=[tok(GLA_DK), tok(GLA_DK), tok(GLA_DK), tok(1024), tok(1024),
                   pl.BlockSpec((chunks, GLA_DK), lambda i: (i, 0))],
        out_shape=[jax.ShapeDtypeStruct((n, GLA_DK), BF16)] * 3 + [jax.ShapeDtypeStruct((n, 1024), BF16)] * 2
                  + [jax.ShapeDtypeStruct((n // GLA_CHUNK, GLA_DK), F32)],
        compiler_params=_params("parallel"),
        name="gla_in",
    )(x, g, w, wlrt, wg3, bg, gn)


def _cast_specs(stacked, layer, steps, step_of):
    rows, cols = stacked.shape[1:]
    slab = rows // steps
    return (pl.BlockSpec((None, slab, cols), lambda *g: (layer, step_of(*g), 0)),
            pl.BlockSpec((None, slab, cols), lambda *g: (0, step_of(*g), 0)),
            jax.ShapeDtypeStruct((1, rows, cols), BF16))


def _gla_core_kernel(qd_ref, ki_ref, ke_ref, v_ref, sg_ref, dec_ref, wu_ref, wd_ref,
                     o_ref, wub_ref, wdb_ref, st_ref, *, chunks):
    @pl.when(pl.program_id(1) == 0)
    def _():
        st_ref[...] = jnp.zeros_like(st_ref)

    wub_ref[...] = wu_ref[...].astype(BF16)
    wdb_ref[...] = wd_ref[...].astype(BF16)
    for _ in _gla_core_steps((qd_ref, ki_ref, ke_ref, v_ref, sg_ref, dec_ref), o_ref, st_ref, chunks):
        pass


def _gla_core(qd, ki, ke, v, sg, dec, wu, wd, layer, batch, seq, tc):
    n = batch * seq
    steps = seq // tc
    chunks = tc // GLA_CHUNK
    tile = lambda b, i: b * steps + i
    tok = lambda c: pl.BlockSpec((tc, c), lambda b, i: (tile(b, i), 0))
    casts = [_cast_specs(w, layer, batch * steps, tile) for w in (wu, wd)]
    return pl.pallas_call(
        functools.partial(_gla_core_kernel, chunks=chunks),
        grid=(batch, steps),
        in_specs=[tok(GLA_DK), tok(GLA_DK), tok(GLA_DK), tok(1024), tok(1024),
                  pl.BlockSpec((chunks, GLA_DK), lambda b, i: (tile(b, i), 0))] + [c[0] for c in casts],
        out_specs=[tok(1024)] + [c[1] for c in casts],
        out_shape=[jax.ShapeDtypeStruct((n, GLA_DV), BF16)] + [c[2] for c in casts],
        scratch_shapes=[pltpu.VMEM((GLA_HEADS, GLA_HV, GLA_HK), F32)],
        compiler_params=_params("parallel", "arbitrary"),
        name="gla_core",
    )(qd, ki, ke, v, sg, dec, wu, wd)


def _swa_in_kernel(x_ref, g_ref, wqt_ref, bqt_ref, wk_ref, bk_ref, wvt_ref, bvt_ref,
                   qt_ref, kp_ref, vt_ref):
    h = _rms(x_ref[...], g_ref[...]).astype(BF16)
    kp_ref[...] = (jnp.dot(h, wk_ref[...], preferred_element_type=F32) + bk_ref[...]).astype(BF16)
    vt_ref[...] = (lax.dot_general(wvt_ref[...], h, NT_DIMS, preferred_element_type=F32)
                   + bvt_ref[...]).astype(BF16)
    qt_ref[...] = (lax.dot_general(wqt_ref[...], h, NT_DIMS, preferred_element_type=F32)
                   + bqt_ref[...]).astype(BF16)


def _swa_in(x, g, wqt, bqt, wk, bk, wvt, bvt, layer, j, tm):
    n = x.shape[0]
    consts = (wqt, bqt, wk, bk, wvt, bvt)
    return pl.pallas_call(
        _swa_in_kernel,
        grid=(n // tm,),
        in_specs=[pl.BlockSpec((tm, D_MODEL), lambda i: (i, 0)), _layer_spec(g, layer)]
                 + [_layer_spec(c, j) for c in consts],
        out_specs=[pl.BlockSpec((SWA_QT_ROWS, tm), lambda i: (0, i)),
                   pl.BlockSpec((tm, 128), lambda i: (i, 0)),
                   pl.BlockSpec((128, tm), lambda i: (0, i))],
        out_shape=[jax.ShapeDtypeStruct((SWA_QT_ROWS, n), BF16),
                   jax.ShapeDtypeStruct((n, 128), BF16),
                   jax.ShapeDtypeStruct((128, n), BF16)],
        compiler_params=_params("parallel"),
        name="swa_in",
    )(x, g, *consts)


def _swa_core_kernel(sink_ref, qt_ref, kp_ref, kpp_ref, vt_ref, vtp_ref, wu_ref, wd_ref,
                     ot_ref, wub_ref, wdb_ref, *, blocks):
    wub_ref[...] = wu_ref[...].astype(BF16)
    wdb_ref[...] = wd_ref[...].astype(BF16)
    first_tile = pl.program_id(1) == 0
    blk = SWA_BLOCK
    sj = lax.broadcasted_iota(jnp.int32, (2 * blk, blk), 0)
    qi = lax.broadcasted_iota(jnp.int32, (2 * blk, blk), 1)
    band = (sj > qi) & (sj <= qi + blk)
    ones = jnp.ones((SWA_ONES_ROWS, 2 * blk), BF16)
    klane = lax.broadcasted_iota(jnp.int32, (2 * blk, SWA_KVH * SWA_HD), 1)
    log2e = math.log2(math.e)

    def heads_of(kh):
        return [kh * SWA_GROUP + g for g in range(SWA_GROUP)]

    def scores(j, kh):
        cols = slice(j * blk, (j + 1) * blk)
        k_prev = kpp_ref[...] if j == 0 else kp_ref[(j - 1) * blk:j * blk, :]
        k_band = jnp.concatenate([k_prev, kp_ref[cols, :]], axis=0)
        k_band = jnp.where((klane // SWA_HD) == kh, k_band, jnp.zeros_like(k_band))
        first = lambda hq: (hq - kh) * SWA_HD
        qt = jnp.concatenate([qt_ref[first(hq):first(hq) + 2 * SWA_HD, cols] for hq in heads_of(kh)], axis=1)
        return jnp.dot(k_band, qt, preferred_element_type=F32)

    def finish(j, kh, s):
        cols = slice(j * blk, (j + 1) * blk)
        valid = band & (jnp.logical_not(first_tile) | (sj >= blk)) if j == 0 else band
        es, mxs = [], []
        for g, hq in enumerate(heads_of(kh)):
            sink = sink_ref[hq] * log2e
            sg = jnp.where(valid, s[:, g * blk:(g + 1) * blk], -jnp.inf)
            mx = jnp.maximum(jnp.max(sg, axis=0, keepdims=True), sink)
            es.append(jnp.exp2(sg - mx).astype(BF16))
            mxs.append(jnp.exp2(sink - mx))
        e = jnp.concatenate(es, axis=1)
        rows = slice(kh * SWA_HD, (kh + 1) * SWA_HD)
        vt_prev = vtp_ref[rows, :] if j == 0 else vt_ref[rows, (j - 1) * blk:j * blk]
        lhs = jnp.concatenate([jnp.concatenate([vt_prev, vt_ref[rows, cols]], axis=1), ones], axis=0)
        o_aug = jnp.dot(lhs, e, preferred_element_type=F32)
        den = o_aug[SWA_HD:SWA_HD + 1, :] + jnp.concatenate(mxs, axis=1)
        o = (o_aug[:SWA_HD, :] / den).astype(BF16)
        for g, hq in enumerate(heads_of(kh)):
            ot_ref[hq * SWA_HD:(hq + 1) * SWA_HD, cols] = o[:, g * blk:(g + 1) * blk]

    units = [(j, kh) for j in range(blocks) for kh in range(SWA_KVH)]
    s_next = scores(*units[0])
    for idx, unit in enumerate(units):
        s = s_next
        if idx + 1 < len(units):
            s_next = scores(*units[idx + 1])
        finish(*unit, s)


def _swa_core(sinks, qt, kp, vt, wu, wd, layer, batch, seq, tq):
    n = batch * seq
    steps = seq // tq
    bpt = tq // SWA_BLOCK
    tile = lambda b, i: b * steps + i
    prev_blk = lambda b, i: jnp.maximum(tile(b, i) * bpt - 1, 0)
    casts = [_cast_specs(w, layer, batch * steps, tile) for w in (wu, wd)]
    return pl.pallas_call(
        functools.partial(_swa_core_kernel, blocks=bpt),
        grid=(batch, steps),
        in_specs=[pl.BlockSpec(memory_space=pltpu.SMEM),
                  pl.BlockSpec((SWA_QT_ROWS, tq), lambda b, i: (0, tile(b, i))),
                  pl.BlockSpec((tq, 128), lambda b, i: (tile(b, i), 0)),
                  pl.BlockSpec((SWA_BLOCK, 128), lambda b, i: (prev_blk(b, i), 0)),
                  pl.BlockSpec((128, tq), lambda b, i: (0, tile(b, i))),
                  pl.BlockSpec((128, SWA_BLOCK), lambda b, i: (0, prev_blk(b, i)))] + [c[0] for c in casts],
        out_specs=[pl.BlockSpec((SWA_QH * SWA_HD, tq), lambda b, i: (0, tile(b, i)))] + [c[1] for c in casts],
        out_shape=[jax.ShapeDtypeStruct((SWA_QH * SWA_HD, n), BF16)] + [c[2] for c in casts],
        compiler_params=_params("parallel", "arbitrary"),
        name="swa_core",
    )(sinks, qt, kp, kp, vt, vt, wu, wd)


MLP_FF_CHUNK = 512


def _dense_kernel(x_ref, o_ref, wo_ref, bo_ref, gpost_ref, g1_ref, g2_ref, wu_ref, wd_ref,
                  y_ref, a_ref, *, o_transposed):
    dims = TN_DIMS if o_transposed else (((1,), (0,)), ((), ()))
    tm = x_ref.shape[0]
    halves = [slice(0, tm // 2), slice(tm // 2, tm)]
    xs, hs = [], []
    for r in halves:
        o = o_ref[:, r] if o_transposed else o_ref[r, :]
        m = lax.dot_general(o, wo_ref[...], dims, preferred_element_type=F32) + bo_ref[...]
        x = x_ref[r, :] + _rms(m, gpost_ref[...])
        xs.append(x)
        hs.append(_rms(x, g1_ref[...]).astype(BF16))
    for r, h in zip(halves, hs):
        for j in range(D_FF // MLP_FF_CHUNK):
            cols = slice(j * MLP_FF_CHUNK, (j + 1) * MLP_FF_CHUNK)
            u = jnp.maximum(jnp.dot(h, wu_ref[:, cols], preferred_element_type=F32), 0.0)
            a_ref[r, cols] = (u * u).astype(BF16)
    for r, x in zip(halves, xs):
        f = jnp.dot(a_ref[r, :], wd_ref[...], preferred_element_type=F32)
        y_ref[r, :] = x + _rms(f, g2_ref[...])


def _dense(x, o, wo, bo, gpost, g1, g2, wu, wd, layer, j, tm, o_transposed):
    n = x.shape[0]
    tok = pl.BlockSpec((tm, D_MODEL), lambda i: (i, 0))
    o_spec = pl.BlockSpec((1024, tm), lambda i: (0, i)) if o_transposed else tok
    return pl.pallas_call(
        functools.partial(_dense_kernel, o_transposed=o_transposed),
        grid=(n // tm,),
        in_specs=[tok, o_spec, _layer_spec(wo, j), _layer_spec(bo, j), _layer_spec(gpost, layer),
                  _layer_spec(g1, layer), _layer_spec(g2, layer), _layer_spec(wu, 0), _layer_spec(wd, 0)],
        out_specs=tok,
        out_shape=jax.ShapeDtypeStruct((n, D_MODEL), F32),
        scratch_shapes=[pltpu.VMEM((tm, D_FF), BF16)],
        compiler_params=_params("parallel"),
        name="dense",
    )(x, o, wo, bo, gpost, g1, g2, wu, wd)


def _split_bf16(w):
    hi = w.astype(BF16)
    return hi, (w - hi.astype(F32)).astype(BF16)


def _gla_weights(w_in, w_gate_up, b_gate_up, g_norm):
    layers = w_in.shape[0]
    w_main = w_in[:, :, :3072].astype(BF16)
    w_lr_t = jnp.swapaxes(w_in[:, :, 3072:], 1, 2).astype(BF16)
    wg_hi, wg_lo = _split_bf16(w_gate_up)
    gn = jnp.tile(g_norm, (1, GLA_HEADS)).reshape(layers, 1, GLA_DV)
    return (w_main, w_lr_t, jnp.concatenate([wg_hi, wg_hi, wg_lo], axis=1),
            b_gate_up.reshape(layers, 1, GLA_DK), gn)


def _swa_weights(w_in, b_in):
    layers = w_in.shape[0]
    nq = SWA_QH * SWA_HD
    qscale = SWA_HD ** -0.5 * math.log2(math.e)
    wqt = jnp.swapaxes((w_in[:, :, :nq] * qscale).astype(BF16), 1, 2)
    bqt = (b_in[:, :nq] * qscale).reshape(layers, nq, 1)
    wk = w_in[:, :, nq:nq + 128].astype(BF16)
    bk = b_in[:, nq:nq + 128].reshape(layers, 1, 128)
    wvt = jnp.swapaxes(w_in[:, :, nq + 128:], 1, 2).astype(BF16)
    bvt = b_in[:, nq + 128:].reshape(layers, 128, 1)
    return wqt, bqt, wk, bk, wvt, bvt


def kernel(x, ln_mix_pre, ln_mix_post, ln_mlp_pre, ln_mlp_post, gla_w_in, gla_w_gate_up, gla_b_gate_up, gla_g_norm, gla_w_out, swa_w_in, swa_b_in, swa_sinks, swa_w_out, swa_b_out, mlp_w_up, mlp_w_down):
    batch, seq, d = x.shape
    n = batch * seq
    depth = ln_mix_pre.shape[0]
    tp, tm = min(PROJ_TILE, n), min(DENSE_TILE, n)
    tc, tq = min(GLA_CORE_TILE, seq), min(SWA_CORE_TILE, seq)
    xf = x.reshape(n, d)
    gain = lambda t: t.reshape(t.shape[0], 1, t.shape[1])
    g_pre, g_post, g1, g2 = gain(ln_mix_pre), gain(ln_mix_post), gain(ln_mlp_pre), gain(ln_mlp_post)
    gla_w = _gla_weights(gla_w_in, gla_w_gate_up, gla_b_gate_up, gla_g_norm)
    swa_w = _swa_weights(swa_w_in, swa_b_in)
    gla_wo, swa_wo = gla_w_out.astype(BF16), swa_w_out.astype(BF16)
    gla_bo = jnp.zeros((gla_w_out.shape[0], 1, D_MODEL), F32)
    swa_bo = gain(swa_b_out)
    for i in range(depth):
        j = i // 2
        if i % 2 == 0:
            qd, ki, ke, v, sg, dec = _gla_in(xf, g_pre, *gla_w, i, j, tp)
            o, wu, wd = _gla_core(qd, ki, ke, v, sg, dec, mlp_w_up, mlp_w_down, i, batch, seq, tc)
            wo, bo = gla_wo, gla_bo
        else:
            qt, kp, vt = _swa_in(xf, g_pre, *swa_w, i, j, tp)
            o, wu, wd = _swa_core(swa_sinks[j], qt, kp, vt, mlp_w_up, mlp_w_down, i, batch, seq, tq)
            wo, bo = swa_wo, swa_bo
        xf = _dense(xf, o, wo, bo, g_post, g1, g2, wu, wd, i, j, tm, o_transposed=(i % 2 == 1))
    return xf.reshape(batch, seq, d)
```

```python
import functools
import math

import jax
import jax.numpy as jnp
from jax import lax
from jax.experimental import pallas as pl
from jax.experimental.pallas import tpu as pltpu

D_MODEL = 1024
NORM_EPS = 1e-6

GLA_HEADS = 4
GLA_DK = 512
GLA_DV = 1024
GLA_HK = 128
GLA_HV = 256
GLA_RANK = 16
GLA_RANK_PAD = 128
GLA_NORMALIZER = 16.0
GLA_CHUNK = 64

SWA_HD = 64
SWA_QH = 16
SWA_GROUP = 8
SWA_KVH = 2
SWA_BLOCK = 128
SWA_QT_ROWS = SWA_QH * SWA_HD
SWA_ONES_ROWS = 16

D_FF = 4096

F32 = jnp.float32
BF16 = jnp.bfloat16

VMEM_LIMIT_BYTES = 56 * 1024 * 1024

PROJ_TILE = 1024
GLA_CORE_TILE = 1024
SWA_CORE_TILE = 512
DENSE_TILE = 1024

LOG2E = math.log2(math.e)

NT_DIMS = (((1,), (1,)), ((), ()))
TN_DIMS = (((0,), (0,)), ((), ()))


def _rms(x, g):
    ms = jnp.mean(x * x, axis=-1, keepdims=True)
    return x * lax.rsqrt(ms + NORM_EPS) * g


def _layer_spec(stacked, layer):
    nd = stacked.ndim - 1
    return pl.BlockSpec((None,) + stacked.shape[1:], lambda *_: (layer,) + (0,) * nd,
                        pipeline_mode=pl.Buffered(1))


def _params(*sem):
    return pltpu.CompilerParams(dimension_semantics=sem, vmem_limit_bytes=VMEM_LIMIT_BYTES)


def _gla_proj_steps(x_ref, g_ref, w_ref, wlrt_ref, wg3_ref, bg_ref, gn_ref, dst, chunks):
    qd_ref, ki_ref, ke_ref, v_ref, sg_ref, dec_ref = dst
    c = GLA_CHUNK
    h = _rms(x_ref[...], g_ref[...]).astype(BF16)
    glr_t = lax.dot_general(wlrt_ref[...], h, NT_DIMS, preferred_element_type=F32)
    hi = glr_t.astype(BF16)
    lo = (glr_t - hi.astype(F32)).astype(BF16)
    z = lax.dot_general(jnp.concatenate([hi, lo, hi], axis=0), wg3_ref[...], TN_DIMS,
                        preferred_element_type=F32) + bg_ref[...]
    la = (jnp.minimum(z, 0.0) - jnp.log(1.0 + jnp.exp(-jnp.abs(z)))) * (1.0 / GLA_NORMALIZER)
    la_hi = la.astype(BF16)
    la_lo = (la - la_hi.astype(F32)).astype(BF16)

    row = lax.broadcasted_iota(jnp.int32, (c, 2 * c), 0)
    col = lax.broadcasted_iota(jnp.int32, (c, 2 * c), 1)
    tril2 = jnp.where((col % c) <= row, 1.0, 0.0).astype(BF16)

    yield
    half = GLA_DV // 2
    qf = jnp.dot(h, w_ref[:, :GLA_DK], preferred_element_type=F32)
    yield
    for p in range(2):
        gate = jnp.dot(h, w_ref[:, 2048 + p * half:2048 + (p + 1) * half], preferred_element_type=F32)
        sg_ref[:, p * half:(p + 1) * half] = (gate * jax.nn.sigmoid(gate)
                                              * gn_ref[:, p * half:(p + 1) * half]).astype(BF16)
        yield
    kf = jnp.dot(h, w_ref[:, GLA_DK:2 * GLA_DK], preferred_element_type=F32)
    yield
    scale = GLA_HK ** -0.5
    bcs = [jnp.dot(tril2, jnp.concatenate([la_hi[ci * c:(ci + 1) * c], la_lo[ci * c:(ci + 1) * c]], axis=0),
                   preferred_element_type=F32) for ci in range(chunks)]
    yield
    for p in range(2):
        v_ref[:, p * half:(p + 1) * half] = jnp.dot(h, w_ref[:, 1024 + p * half:1024 + (p + 1) * half],
                                                    preferred_element_type=F32).astype(BF16)
        for ci in range(p * chunks // 2, (p + 1) * chunks // 2):
            rows = slice(ci * c, (ci + 1) * c)
            bc = bcs[ci] * LOG2E
            b_last = bc[c - 1:c, :]
            k = kf[rows, :]
            qd_ref[rows, :] = (qf[rows, :] * (scale * jnp.exp2(bc))).astype(BF16)
            ki_ref[rows, :] = (k * jnp.exp2(-bc)).astype(BF16)
            ke_ref[rows, :] = (k * jnp.exp2(b_last - bc)).astype(BF16)
            dec_ref[ci:ci + 1, :] = jnp.exp2(b_last)
        yield


def _gla_core_steps(src, o_ref, st_ref, chunks):
    qd_ref, ki_ref, ke_ref, v_ref, sg_ref, dec_ref = src
    c = GLA_CHUNK
    row = lax.broadcasted_iota(jnp.int32, (c, c), 0)
    col = lax.broadcasted_iota(jnp.int32, (c, c), 1)
    causal = col <= row

    def scores(ci):
        rows = slice(ci * c, (ci + 1) * c)
        return [lax.dot_general(qd_ref[rows, h * GLA_HK:(h + 1) * GLA_HK],
                                ki_ref[rows, h * GLA_HK:(h + 1) * GLA_HK], NT_DIMS,
                                preferred_element_type=F32) for h in range(GLA_HEADS)]

    att_next = scores(0)
    for ci in range(chunks):
        rows = slice(ci * c, (ci + 1) * c)
        att_all = att_next
        if ci + 1 < chunks:
            att_next = scores(ci + 1)
        decay = dec_ref[ci:ci + 1, :]
        for h in range(GLA_HEADS):
            ks = slice(h * GLA_HK, (h + 1) * GLA_HK)
            vs = slice(h * GLA_HV, (h + 1) * GLA_HV)
            v = v_ref[rows, vs]
            att = jnp.where(causal, att_all[h], 0.0).astype(BF16)
            st = st_ref[h]
            o = (jnp.dot(att, v, preferred_element_type=F32)
                 + lax.dot_general(qd_ref[rows, ks], st.astype(BF16), NT_DIMS,
                                   preferred_element_type=F32))
            st_ref[h] = st * decay[:, ks] + lax.dot_general(v, ke_ref[rows, ks], TN_DIMS,
                                                            preferred_element_type=F32)
            ms = jnp.mean(o * o, axis=-1, keepdims=True)
            o_ref[rows, vs] = (o * lax.rsqrt(ms + NORM_EPS) * sg_ref[rows, vs].astype(F32)).astype(BF16)
        yield


def _gla_in_kernel(x_ref, g_ref, w_ref, wlrt_ref, wg3_ref, bg_ref, gn_ref, *dst, chunks):
    for _ in _gla_proj_steps(x_ref, g_ref, w_ref, wlrt_ref, wg3_ref, bg_ref, gn_ref, dst, chunks):
        pass


def _gla_in(x, g, w, wlrt, wg3, bg, gn, layer, j, tm):
    n = x.shape[0]
    chunks = tm // GLA_CHUNK
    tok = lambda c: pl.BlockSpec((tm, c), lambda i: (i, 0))
    return pl.pallas_call(
        functools.partial(_gla_in_kernel, chunks=chunks),
        grid=(n // tm,),
        in_specs=[tok(D_MODEL), _layer_spec(g, layer), _layer_spec(w, j), _layer_spec(wlrt, j),
                  _layer_spec(wg3, j), _layer_spec(bg, j), _layer_spec(gn, j)],
        out_specs=[tok(GLA_DK), tok(GLA_DK), tok(GLA_DK), tok(1024), tok(1024),
                   pl.BlockSpec((chunks, GLA_DK), lambda i: (i, 0))],
        out_shape=[jax.ShapeDtypeStruct((n, GLA_DK), BF16)] * 3 + [jax.ShapeDtypeStruct((n, 1024), BF16)] * 2
                  + [jax.ShapeDtypeStruct((n // GLA_CHUNK, GLA_DK), F32)],
        compiler_params=_params("parallel"),
        name="gla_in",
    )(x, g, w, wlrt, wg3, bg, gn)


def _cast_specs(stacked, layer, steps, step_of):
    rows, cols = stacked.shape[1:]
    slab = rows // steps
    return (pl.BlockSpec((None, slab, cols), lambda *g: (layer, step_of(*g), 0)),
            pl.BlockSpec((None, slab, cols), lambda *g: (0, step_of(*g), 0)),
            jax.ShapeDtypeStruct((1, rows, cols), BF16))


def _gla_core_kernel(qd_ref, ki_ref, ke_ref, v_ref, sg_ref, dec_ref, wu_ref, wd_ref,
                     o_ref, wub_ref, wdb_ref, st_ref, *, chunks):
    @pl.when(pl.program_id(1) == 0)
    def _():
        st_ref[...] = jnp.zeros_like(st_ref)

    wub_ref[...] = wu_ref[...].astype(BF16)
    wdb_ref[...] = wd_ref[...].astype(BF16)
    for _ in _gla_core_steps((qd_ref, ki_ref, ke_ref, v_ref, sg_ref, dec_ref), o_ref, st_ref, chunks):
        pass


def _gla_core(qd, ki, ke, v, sg, dec, wu, wd, layer, batch, seq, tc):
    n = batch * seq
    steps = seq // tc
    chunks = tc // GLA_CHUNK
    tile = lambda b, i: b * steps + i
    tok = lambda c: pl.BlockSpec((tc, c), lambda b, i: (tile(b, i), 0))
    casts = [_cast_specs(w, layer, batch * steps, tile) for w in (wu, wd)]
    return pl.pallas_call(
        functools.partial(_gla_core_kernel, chunks=chunks),
        grid=(batch, steps),
        in_specs=[tok(GLA_DK), tok(GLA_DK), tok(GLA_DK), tok(1024), tok(1024),
                  pl.BlockSpec((chunks, GLA_DK), lambda b, i: (tile(b, i), 0))] + [c[0] for c in casts],
        out_specs=[tok(1024)] + [c[1] for c in casts],
        out_shape=[jax.ShapeDtypeStruct((n, GLA_DV), BF16)] + [c[2] for c in casts],
        scratch_shapes=[pltpu.VMEM((GLA_HEADS, GLA_HV, GLA_HK), F32)],
        compiler_params=_params("parallel", "arbitrary"),
        name="gla_core",
    )(qd, ki, ke, v, sg, dec, wu, wd)


def _swa_in_kernel(x_ref, g_ref, wqt_ref, bqt_ref, wk_ref, bk_ref, wvt_ref, bvt_ref,
                   qt_ref, kp_ref, vt_ref):
    h = _rms(x_ref[...], g_ref[...]).astype(BF16)
    kp_ref[...] = (jnp.dot(h, wk_ref[...], preferred_element_type=F32) + bk_ref[...]).astype(BF16)
    vt_ref[...] = (lax.dot_general(wvt_ref[...], h, NT_DIMS, preferred_element_type=F32)
                   + bvt_ref[...]).astype(BF16)
    qt_ref[...] = (lax.dot_general(wqt_ref[...], h, NT_DIMS, preferred_element_type=F32)
                   + bqt_ref[...]).astype(BF16)


def _swa_in(x, g, wqt, bqt, wk, bk, wvt, bvt, layer, j, tm):
    n = x.shape[0]
    consts = (wqt, bqt, wk, bk, wvt, bvt)
    return pl.pallas_call(
        _swa_in_kernel,
        grid=(n // tm,),
        in_specs=[pl.BlockSpec((tm, D_MODEL), lambda i: (i, 0)), _layer_spec(g, layer)]
                 + [_layer_spec(c, j) for c in consts],
        out_specs=[pl.BlockSpec((SWA_QT_ROWS, tm), lambda i: (0, i)),
                   pl.BlockSpec((tm, 128), lambda i: (i, 0)),
                   pl.BlockSpec((128, tm), lambda i: (0, i))],
        out_shape=[jax.ShapeDtypeStruct((SWA_QT_ROWS, n), BF16),
                   jax.ShapeDtypeStruct((n, 128), BF16),
                   jax.ShapeDtypeStruct((128, n), BF16)],
        compiler_params=_params("parallel"),
        name="swa_in",
    )(x, g, *consts)


def _swa_core_kernel(sink_ref, qt_ref, kp_ref, kpp_ref, vt_ref, vtp_ref, wu_ref, wd_ref,
                     ot_ref, wub_ref, wdb_ref, *, blocks):
    wub_ref[...] = wu_ref[...].astype(BF16)
    wdb_ref[...] = wd_ref[...].astype(BF16)
    first_tile = pl.program_id(1) == 0
    blk = SWA_BLOCK
    sj = lax.broadcasted_iota(jnp.int32, (blk, blk), 0)
    qi = lax.broadcasted_iota(jnp.int32, (blk, blk), 1)
    prev_valid = sj > qi
    prev_mask = jnp.where(prev_valid, 1.0, 0.0).astype(BF16)
    cur_mask = jnp.where(prev_valid, 0.0, 1.0).astype(BF16)
    ones = jnp.ones((SWA_ONES_ROWS, 2 * blk), BF16)
    klane = lax.broadcasted_iota(jnp.int32, (2 * blk, SWA_KVH * SWA_HD), 1)

    def heads_of(kh):
        return [kh * SWA_GROUP + g for g in range(SWA_GROUP)]

    def scores(j, kh):
        cols = slice(j * blk, (j + 1) * blk)
        k_prev = kpp_ref[...] if j == 0 else kp_ref[(j - 1) * blk:j * blk, :]
        k_band = jnp.concatenate([k_prev, kp_ref[cols, :]], axis=0)
        k_band = jnp.where((klane // SWA_HD) == kh, k_band, jnp.zeros_like(k_band))
        first = lambda hq: (hq - kh) * SWA_HD
        qt = jnp.concatenate([qt_ref[first(hq):first(hq) + 2 * SWA_HD, cols] for hq in heads_of(kh)], axis=1)
        return jnp.dot(k_band, qt, preferred_element_type=F32)

    def finish(j, kh, s):
        cols = slice(j * blk, (j + 1) * blk)
        es, mxs = [], []
        for g, hq in enumerate(heads_of(kh)):
            sink = sink_ref[hq] * LOG2E
            s_prev = s[:blk, g * blk:(g + 1) * blk]
            if j == 0:
                s_prev = jnp.where(first_tile, -jnp.inf, s_prev)
            sm = jnp.where(prev_valid, s_prev, s[blk:, g * blk:(g + 1) * blk])
            mx = jnp.maximum(jnp.max(sm, axis=0, keepdims=True), sink)
            em = jnp.exp2(sm - mx).astype(BF16)
            es.append(jnp.concatenate([em * prev_mask, em * cur_mask], axis=0))
            mxs.append(jnp.exp2(sink - mx))
        e = jnp.concatenate(es, axis=1)
        rows = slice(kh * SWA_HD, (kh + 1) * SWA_HD)
        vt_prev = vtp_ref[rows, :] if j == 0 else vt_ref[rows, (j - 1) * blk:j * blk]
        lhs = jnp.concatenate([jnp.concatenate([vt_prev, vt_ref[rows, cols]], axis=1), ones], axis=0)
        o_aug = jnp.dot(lhs, e, preferred_element_type=F32)
        den = o_aug[SWA_HD:SWA_HD + 1, :] + jnp.concatenate(mxs, axis=1)
        o = (o_aug[:SWA_HD, :] / den).astype(BF16)
        for g, hq in enumerate(heads_of(kh)):
            ot_ref[hq * SWA_HD:(hq + 1) * SWA_HD, cols] = o[:, g * blk:(g + 1) * blk]

    units = [(j, kh) for j in range(blocks) for kh in range(SWA_KVH)]
    s_next = scores(*units[0])
    for idx, unit in enumerate(units):
        s = s_next
        if idx + 1 < len(units):
            s_next = scores(*units[idx + 1])
        finish(*unit, s)


def _swa_core(sinks, qt, kp, vt, wu, wd, layer, batch, seq, tq):
    n = batch * seq
    steps = seq // tq
    bpt = tq // SWA_BLOCK
    tile = lambda b, i: b * steps + i
    prev_blk = lambda b, i: jnp.maximum(tile(b, i) * bpt - 1, 0)
    casts = [_cast_specs(w, layer, batch * steps, tile) for w in (wu, wd)]
    return pl.pallas_call(
        functools.partial(_swa_core_kernel, blocks=bpt),
        grid=(batch, steps),
        in_specs=[pl.BlockSpec(memory_space=pltpu.SMEM),
                  pl.BlockSpec((SWA_QT_ROWS, tq), lambda b, i: (0, tile(b, i))),
                  pl.BlockSpec((tq, 128), lambda b, i: (tile(b, i), 0)),
                  pl.BlockSpec((SWA_BLOCK, 128), lambda b, i: (prev_blk(b, i), 0)),
                  pl.BlockSpec((128, tq), lambda b, i: (0, tile(b, i))),
                  pl.BlockSpec((128, SWA_BLOCK), lambda b, i: (0, prev_blk(b, i)))] + [c[0] for c in casts],
        out_specs=[pl.BlockSpec((SWA_QH * SWA_HD, tq), lambda b, i: (0, tile(b, i)))] + [c[1] for c in casts],
        out_shape=[jax.ShapeDtypeStruct((SWA_QH * SWA_HD, n), BF16)] + [c[2] for c in casts],
        compiler_params=_params("parallel", "arbitrary"),
        name="swa_core",
    )(sinks, qt, kp, kp, vt, vt, wu, wd)


MLP_FF_CHUNK = 512


def _dense_kernel(x_ref, o_ref, wo_ref, bo_ref, gpost_ref, g1_ref, g2_ref, wu_ref, wd_ref,
                  y_ref, a_ref, *, o_transposed):
    dims = TN_DIMS if o_transposed else (((1,), (0,)), ((), ()))
    tm = x_ref.shape[0]
    halves = [slice(0, tm // 2), slice(tm // 2, tm)]
    xs, hs = [], []
    for r in halves:
        o = o_ref[:, r] if o_transposed else o_ref[r, :]
        m = lax.dot_general(o, wo_ref[...], dims, preferred_element_type=F32) + bo_ref[...]
        x = x_ref[r, :] + _rms(m, gpost_ref[...])
        xs.append(x)
        hs.append(_rms(x, g1_ref[...]).astype(BF16))
    for r, h in zip(halves, hs):
        for j in range(D_FF // MLP_FF_CHUNK):
            cols = slice(j * MLP_FF_CHUNK, (j + 1) * MLP_FF_CHUNK)
            u = jnp.maximum(jnp.dot(h, wu_ref[:, cols], preferred_element_type=F32), 0.0)
            a_ref[r, cols] = (u * u).astype(BF16)
    for r, x in zip(halves, xs):
        f = jnp.dot(a_ref[r, :], wd_ref[...], preferred_element_type=F32)
        y_ref[r, :] = x + _rms(f, g2_ref[...])


def _dense(x, o, wo, bo, gpost, g1, g2, wu, wd, layer, j, tm, o_transposed):
    n = x.shape[0]
    tok = pl.BlockSpec((tm, D_MODEL), lambda i: (i, 0))
    o_spec = pl.BlockSpec((1024, tm), lambda i: (0, i)) if o_transposed else tok
    return pl.pallas_call(
        functools.partial(_dense_kernel, o_transposed=o_transposed),
        grid=(n // tm,),
        in_specs=[tok, o_spec, _layer_spec(wo, j), _layer_spec(bo, j), _layer_spec(gpost, layer),
                  _layer_spec(g1, layer), _layer_spec(g2, layer), _layer_spec(wu, 0), _layer_spec(wd, 0)],
        out_specs=tok,
        out_shape=jax.ShapeDtypeStruct((n, D_MODEL), F32),
        scratch_shapes=[pltpu.VMEM((tm, D_FF), BF16)],
        compiler_params=_params("parallel"),
        name="dense",
    )(x, o, wo, bo, gpost, g1, g2, wu, wd)


def _split_bf16(w):
    hi = w.astype(BF16)
    return hi, (w - hi.astype(F32)).astype(BF16)


def _gla_weights(w_in, w_gate_up, b_gate_up, g_norm):
    layers = w_in.shape[0]
    w_main = w_in[:, :, :3072].astype(BF16)
    w_lr_t = jnp.swapaxes(w_in[:, :, 3072:], 1, 2).astype(BF16)
    wg_hi, wg_lo = _split_bf16(w_gate_up)
    gn = jnp.tile(g_norm, (1, GLA_HEADS)).reshape(layers, 1, GLA_DV)
    return (w_main, w_lr_t, jnp.concatenate([wg_hi, wg_hi, wg_lo], axis=1),
            b_gate_up.reshape(layers, 1, GLA_DK), gn)


def _swa_weights(w_in, b_in):
    layers = w_in.shape[0]
    nq = SWA_QH * SWA_HD
    qscale = SWA_HD ** -0.5 * LOG2E
    wqt = jnp.swapaxes((w_in[:, :, :nq] * qscale).astype(BF16), 1, 2)
    bqt = (b_in[:, :nq] * qscale).reshape(layers, nq, 1)
    wk = w_in[:, :, nq:nq + 128].astype(BF16)
    bk = b_in[:, nq:nq + 128].reshape(layers, 1, 128)
    wvt = jnp.swapaxes(w_in[:, :, nq + 128:], 1, 2).astype(BF16)
    bvt = b_in[:, nq + 128:].reshape(layers, 128, 1)
    return wqt, bqt, wk, bk, wvt, bvt


def kernel(x, ln_mix_pre, ln_mix_post, ln_mlp_pre, ln_mlp_post, gla_w_in, gla_w_gate_up, gla_b_gate_up, gla_g_norm, gla_w_out, swa_w_in, swa_b_in, swa_sinks, swa_w_out, swa_b_out, mlp_w_up, mlp_w_down):
    batch, seq, d = x.shape
    n = batch * seq
    depth = ln_mix_pre.shape[0]
    tp, tm = min(PROJ_TILE, n), min(DENSE_TILE, n)
    tc, tq = min(GLA_CORE_TILE, seq), min(SWA_CORE_TILE, seq)
    xf = x.reshape(n, d)
    gain = lambda t: t.reshape(t.shape[0], 1, t.shape[1])
    g_pre, g_post, g1, g2 = gain(ln_mix_pre), gain(ln_mix_post), gain(ln_mlp_pre), gain(ln_mlp_post)
    gla_w = _gla_weights(gla_w_in, gla_w_gate_up, gla_b_gate_up, gla_g_norm)
    swa_w = _swa_weights(swa_w_in, swa_b_in)
    gla_wo, swa_wo = gla_w_out.astype(BF16), swa_w_out.astype(BF16)
    gla_bo = jnp.zeros((gla_w_out.shape[0], 1, D_MODEL), F32)
    swa_bo = gain(swa_b_out)
    for i in range(depth):
        j = i // 2
        if i % 2 == 0:
            qd, ki, ke, v, sg, dec = _gla_in(xf, g_pre, *gla_w, i, j, tp)
            o, wu, wd = _gla_core(qd, ki, ke, v, sg, dec, mlp_w_up, mlp_w_down, i, batch, seq, tc)
            wo, bo = gla_wo, gla_bo
        else:
            qt, kp, vt = _swa_in(xf, g_pre, *swa_w, i, j, tp)
            o, wu, wd = _swa_core(swa_sinks[j], qt, kp, vt, mlp_w_up, mlp_w_down, i, batch, seq, tq)
            wo, bo = swa_wo, swa_bo
        xf = _dense(xf, o, wo, bo, g_post, g1, g2, wu, wd, i, j, tm, o_transposed=(i % 2 == 1))
    return xf.reshape(batch, seq, d)
```

```python
import functools
import math

import jax
import jax.numpy as jnp
from jax import lax
from jax.experimental import pallas as pl
from jax.experimental.pallas import tpu as pltpu

D_MODEL = 1024
NORM_EPS = 1e-6

GLA_HEADS = 4
GLA_DK = 512
GLA_DV = 1024
GLA_HK = 128
GLA_HV = 256
GLA_RANK = 16
GLA_RANK_PAD = 128
GLA_NORMALIZER = 16.0
GLA_CHUNK = 64

SWA_HD = 64
SWA_QH = 16
SWA_GROUP = 8
SWA_KVH = 2
SWA_BLOCK = 128
SWA_QT_ROWS = SWA_QH * SWA_HD
SWA_ONES_ROWS = 16

D_FF = 4096

F32 = jnp.float32
BF16 = jnp.bfloat16

VMEM_LIMIT_BYTES = 56 * 1024 * 1024

PROJ_TILE = 1024
GLA_CORE_TILE = 2048
SWA_CORE_TILE = 512
DENSE_TILE = 1024

LOG2E = math.log2(math.e)

NT_DIMS = (((1,), (1,)), ((), ()))
TN_DIMS = (((0,), (0,)), ((), ()))


def _rms(x, g):
    ms = jnp.mean(x * x, axis=-1, keepdims=True)
    return x * lax.rsqrt(ms + NORM_EPS) * g


def _layer_spec(stacked, layer):
    nd = stacked.ndim - 1
    return pl.BlockSpec((None,) + stacked.shape[1:], lambda *_: (layer,) + (0,) * nd,
                        pipeline_mode=pl.Buffered(1))


def _params(*sem):
    return pltpu.CompilerParams(dimension_semantics=sem, vmem_limit_bytes=VMEM_LIMIT_BYTES)


def _gla_proj_steps(x_ref, g_ref, w_ref, wlrt_ref, wg3_ref, bg_ref, gn_ref, dst, chunks):
    qd_ref, ki_ref, ke_ref, v_ref, sg_ref, dec_ref = dst
    c = GLA_CHUNK
    h = _rms(x_ref[...], g_ref[...]).astype(BF16)
    glr_t = lax.dot_general(wlrt_ref[...], h, NT_DIMS, preferred_element_type=F32)
    hi = glr_t.astype(BF16)
    lo = (glr_t - hi.astype(F32)).astype(BF16)
    z = lax.dot_general(jnp.concatenate([hi, lo, hi], axis=0), wg3_ref[...], TN_DIMS,
                        preferred_element_type=F32) + bg_ref[...]
    la = (jnp.minimum(z, 0.0) - jnp.log(1.0 + jnp.exp(-jnp.abs(z)))) * (1.0 / GLA_NORMALIZER)
    la_hi = la.astype(BF16)
    la_lo = (la - la_hi.astype(F32)).astype(BF16)

    row = lax.broadcasted_iota(jnp.int32, (c, 2 * c), 0)
    col = lax.broadcasted_iota(jnp.int32, (c, 2 * c), 1)
    tril2 = jnp.where((col % c) <= row, 1.0, 0.0).astype(BF16)

    yield
    half = GLA_DV // 2
    qf = jnp.dot(h, w_ref[:, :GLA_DK], preferred_element_type=F32)
    yield
    for p in range(2):
        gate = jnp.dot(h, w_ref[:, 2048 + p * half:2048 + (p + 1) * half], preferred_element_type=F32)
        sg_ref[:, p * half:(p + 1) * half] = (gate * jax.nn.sigmoid(gate)
                                              * gn_ref[:, p * half:(p + 1) * half]).astype(BF16)
        yield
    kf = jnp.dot(h, w_ref[:, GLA_DK:2 * GLA_DK], preferred_element_type=F32)
    yield
    scale = GLA_HK ** -0.5
    bcs = [jnp.dot(tril2, jnp.concatenate([la_hi[ci * c:(ci + 1) * c], la_lo[ci * c:(ci + 1) * c]], axis=0),
                   preferred_element_type=F32) for ci in range(chunks)]
    yield
    for p in range(2):
        v_ref[:, p * half:(p + 1) * half] = jnp.dot(h, w_ref[:, 1024 + p * half:1024 + (p + 1) * half],
                                                    preferred_element_type=F32).astype(BF16)
        for ci in range(p * chunks // 2, (p + 1) * chunks // 2):
            rows = slice(ci * c, (ci + 1) * c)
            bc = bcs[ci] * LOG2E
            b_last = bc[c - 1:c, :]
            k = kf[rows, :]
            qd_ref[rows, :] = (qf[rows, :] * (scale * jnp.exp2(bc))).astype(BF16)
            ki_ref[rows, :] = (k * jnp.exp2(-bc)).astype(BF16)
            ke_ref[rows, :] = (k * jnp.exp2(b_last - bc)).astype(BF16)
            dec_ref[ci:ci + 1, :] = jnp.exp2(b_last)
        yield


def _gla_core_steps(src, o_ref, st_ref, chunks):
    qd_ref, ki_ref, ke_ref, v_ref, sg_ref, dec_ref = src
    c = GLA_CHUNK
    row = lax.broadcasted_iota(jnp.int32, (c, c), 0)
    col = lax.broadcasted_iota(jnp.int32, (c, c), 1)
    causal = col <= row

    def scores(ci):
        rows = slice(ci * c, (ci + 1) * c)
        return [lax.dot_general(qd_ref[rows, h * GLA_HK:(h + 1) * GLA_HK],
                                ki_ref[rows, h * GLA_HK:(h + 1) * GLA_HK], NT_DIMS,
                                preferred_element_type=F32) for h in range(GLA_HEADS)]

    att_next = scores(0)
    for ci in range(chunks):
        rows = slice(ci * c, (ci + 1) * c)
        att_all = att_next
        if ci + 1 < chunks:
            att_next = scores(ci + 1)
        decay = dec_ref[ci:ci + 1, :]
        for h in range(GLA_HEADS):
            ks = slice(h * GLA_HK, (h + 1) * GLA_HK)
            vs = slice(h * GLA_HV, (h + 1) * GLA_HV)
            v = v_ref[rows, vs]
            att = jnp.where(causal, att_all[h], 0.0).astype(BF16)
            st = st_ref[h]
            o = (jnp.dot(att, v, preferred_element_type=F32)
                 + lax.dot_general(qd_ref[rows, ks], st.astype(BF16), NT_DIMS,
                                   preferred_element_type=F32))
            st_ref[h] = st * decay[:, ks] + lax.dot_general(v, ke_ref[rows, ks], TN_DIMS,
                                                            preferred_element_type=F32)
            ms = jnp.mean(o * o, axis=-1, keepdims=True)
            o_ref[rows, vs] = (o * lax.rsqrt(ms + NORM_EPS) * sg_ref[rows, vs].astype(F32)).astype(BF16)
        yield


def _gla_in_kernel(x_ref, g_ref, w_ref, wlrt_ref, wg3_ref, bg_ref, gn_ref, *dst, chunks):
    for _ in _gla_proj_steps(x_ref, g_ref, w_ref, wlrt_ref, wg3_ref, bg_ref, gn_ref, dst, chunks):
        pass


def _gla_in(x, g, w, wlrt, wg3, bg, gn, layer, j, tm):
    n = x.shape[0]
    chunks = tm // GLA_CHUNK
    tok = lambda c: pl.BlockSpec((tm, c), lambda i: (i, 0))
    return pl.pallas_call(
        functools.partial(_gla_in_kernel, chunks=chunks),
        grid=(n // tm,),
        in_specs=[tok(D_MODEL), _layer_spec(g, layer), _layer_spec(w, j), _layer_spec(wlrt, j),
                  _layer_spec(wg3, j), _layer_spec(bg, j), _layer_spec(gn, j)],
        out_specs=[tok(GLA_DK), tok(GLA_DK), tok(GLA_DK), tok(1024), tok(1024),
                   pl.BlockSpec((chunks, GLA_DK), lambda i: (i, 0))],
        out_shape=[jax.ShapeDtypeStruct((n, GLA_DK), BF16)] * 3 + [jax.ShapeDtypeStruct((n, 1024), BF16)] * 2
                  + [jax.ShapeDtypeStruct((n // GLA_CHUNK, GLA_DK), F32)],
        compiler_params=_params("parallel"),
        name="gla_in",
    )(x, g, w, wlrt, wg3, bg, gn)


def _cast_specs(stacked, layer, steps, step_of):
    rows, cols = stacked.shape[1:]
    slab = rows // steps
    return (pl.BlockSpec((None, slab, cols), lambda *g: (layer, step_of(*g), 0)),
            pl.BlockSpec((None, slab, cols), lambda *g: (0, step_of(*g), 0)),
            jax.ShapeDtypeStruct((1, rows, cols), BF16))


def _gla_core_kernel(qd_ref, ki_ref, ke_ref, v_ref, sg_ref, dec_ref, wu_ref, wd_ref,
                     o_ref, wub_ref, wdb_ref, st_ref, *, chunks):
    @pl.when(pl.program_id(1) == 0)
    def _():
        st_ref[...] = jnp.zeros_like(st_ref)

    wub_ref[...] = wu_ref[...].astype(BF16)
    wdb_ref[...] = wd_ref[...].astype(BF16)
    for _ in _gla_core_steps((qd_ref, ki_ref, ke_ref, v_ref, sg_ref, dec_ref), o_ref, st_ref, chunks):
        pass


def _gla_core(qd, ki, ke, v, sg, dec, wu, wd, layer, batch, seq, tc):
    n = batch * seq
    steps = seq // tc
    chunks = tc // GLA_CHUNK
    tile = lambda b, i: b * steps + i
    tok = lambda c: pl.BlockSpec((tc, c), lambda b, i: (tile(b, i), 0))
    casts = [_cast_specs(w, layer, batch * steps, tile) for w in (wu, wd)]
    return pl.pallas_call(
        functools.partial(_gla_core_kernel, chunks=chunks),
        grid=(batch, steps),
        in_specs=[tok(GLA_DK), tok(GLA_DK), tok(GLA_DK), tok(1024), tok(1024),
                  pl.BlockSpec((chunks, GLA_DK), lambda b, i: (tile(b, i), 0))] + [c[0] for c in casts],
        out_specs=[tok(1024)] + [c[1] for c in casts],
        out_shape=[jax.ShapeDtypeStruct((n, GLA_DV), BF16)] + [c[2] for c in casts],
        scratch_shapes=[pltpu.VMEM((GLA_HEADS, GLA_HV, GLA_HK), F32)],
        compiler_params=_params("parallel", "arbitrary"),
        name="gla_core",
    )(qd, ki, ke, v, sg, dec, wu, wd)


def _swa_in_kernel(x_ref, g_ref, wqt_ref, bqt_ref, wk_ref, bk_ref, wvt_ref, bvt_ref,
                   qt_ref, kp_ref, vt_ref):
    h = _rms(x_ref[...], g_ref[...]).astype(BF16)
    kp_ref[...] = (jnp.dot(h, wk_ref[...], preferred_element_type=F32) + bk_ref[...]).astype(BF16)
    vt_ref[...] = (lax.dot_general(wvt_ref[...], h, NT_DIMS, preferred_element_type=F32)
                   + bvt_ref[...]).astype(BF16)
    qt_ref[...] = (lax.dot_general(wqt_ref[...], h, NT_DIMS, preferred_element_type=F32)
                   + bqt_ref[...]).astype(BF16)


def _swa_in(x, g, wqt, bqt, wk, bk, wvt, bvt, layer, j, tm):
    n = x.shape[0]
    consts = (wqt, bqt, wk, bk, wvt, bvt)
    return pl.pallas_call(
        _swa_in_kernel,
        grid=(n // tm,),
        in_specs=[pl.BlockSpec((tm, D_MODEL), lambda i: (i, 0)), _layer_spec(g, layer)]
                 + [_layer_spec(c, j) for c in consts],
        out_specs=[pl.BlockSpec((SWA_QT_ROWS, tm), lambda i: (0, i)),
                   pl.BlockSpec((tm, 128), lambda i: (i, 0)),
                   pl.BlockSpec((128, tm), lambda i: (0, i))],
        out_shape=[jax.ShapeDtypeStruct((SWA_QT_ROWS, n), BF16),
                   jax.ShapeDtypeStruct((n, 128), BF16),
                   jax.ShapeDtypeStruct((128, n), BF16)],
        compiler_params=_params("parallel"),
        name="swa_in",
    )(x, g, *consts)


def _swa_core_kernel(sink_ref, qt_ref, kp_ref, kpp_ref, vt_ref, vtp_ref, wu_ref, wd_ref,
                     ot_ref, wub_ref, wdb_ref, *, blocks):
    wub_ref[...] = wu_ref[...].astype(BF16)
    wdb_ref[...] = wd_ref[...].astype(BF16)
    first_tile = pl.program_id(1) == 0
    blk = SWA_BLOCK
    sj = lax.broadcasted_iota(jnp.int32, (blk, blk), 0)
    qi = lax.broadcasted_iota(jnp.int32, (blk, blk), 1)
    prev_valid = sj > qi
    prev_mask = jnp.where(prev_valid, 1.0, 0.0).astype(BF16)
    cur_mask = jnp.where(prev_valid, 0.0, 1.0).astype(BF16)
    ones = jnp.ones((SWA_ONES_ROWS, 2 * blk), BF16)
    klane = lax.broadcasted_iota(jnp.int32, (2 * blk, SWA_KVH * SWA_HD), 1)

    def heads_of(kh):
        return [kh * SWA_GROUP + g for g in range(SWA_GROUP)]

    def scores(j, kh):
        cols = slice(j * blk, (j + 1) * blk)
        k_prev = kpp_ref[...] if j == 0 else kp_ref[(j - 1) * blk:j * blk, :]
        k_band = jnp.concatenate([k_prev, kp_ref[cols, :]], axis=0)
        k_band = jnp.where((klane // SWA_HD) == kh, k_band, jnp.zeros_like(k_band))
        first = lambda hq: (hq - kh) * SWA_HD
        qt = jnp.concatenate([qt_ref[first(hq):first(hq) + 2 * SWA_HD, cols] for hq in heads_of(kh)], axis=1)
        return jnp.dot(k_band, qt, preferred_element_type=F32)

    def finish(j, kh, s):
        cols = slice(j * blk, (j + 1) * blk)
        es, mxs = [], []
        for g, hq in enumerate(heads_of(kh)):
            sink = sink_ref[hq] * LOG2E
            s_prev = s[:blk, g * blk:(g + 1) * blk]
            if j == 0:
                s_prev = jnp.where(first_tile, -jnp.inf, s_prev)
            sm = jnp.where(prev_valid, s_prev, s[blk:, g * blk:(g + 1) * blk])
            mx = jnp.maximum(jnp.max(sm, axis=0, keepdims=True), sink)
            em = jnp.exp2(sm - mx).astype(BF16)
            es.append(jnp.concatenate([em * prev_mask, em * cur_mask], axis=0))
            mxs.append(jnp.exp2(sink - mx))
        e = jnp.concatenate(es, axis=1)
        rows = slice(kh * SWA_HD, (kh + 1) * SWA_HD)
        vt_prev = vtp_ref[rows, :] if j == 0 else vt_ref[rows, (j - 1) * blk:j * blk]
        lhs = jnp.concatenate([jnp.concatenate([vt_prev, vt_ref[rows, cols]], axis=1), ones], axis=0)
        o_aug = jnp.dot(lhs, e, preferred_element_type=F32)
        den = o_aug[SWA_HD:SWA_HD + 1, :] + jnp.concatenate(mxs, axis=1)
        o = (o_aug[:SWA_HD, :] / den).astype(BF16)
        for g, hq in enumerate(heads_of(kh)):
            ot_ref[hq * SWA_HD:(hq + 1) * SWA_HD, cols] = o[:, g * blk:(g + 1) * blk]

    units = [(j, kh) for j in range(blocks) for kh in range(SWA_KVH)]
    s_next = scores(*units[0])
    for idx, unit in enumerate(units):
        s = s_next
        if idx + 1 < len(units):
            s_next = scores(*units[idx + 1])
        finish(*unit, s)


def _swa_core(sinks, qt, kp, vt, wu, wd, layer, batch, seq, tq):
    n = batch * seq
    steps = seq // tq
    bpt = tq // SWA_BLOCK
    tile = lambda b, i: b * steps + i
    prev_blk = lambda b, i: jnp.maximum(tile(b, i) * bpt - 1, 0)
    casts = [_cast_specs(w, layer, batch * steps, tile) for w in (wu, wd)]
    return pl.pallas_call(
        functools.partial(_swa_core_kernel, blocks=bpt),
        grid=(batch, steps),
        in_specs=[pl.BlockSpec(memory_space=pltpu.SMEM),
                  pl.BlockSpec((SWA_QT_ROWS, tq), lambda b, i: (0, tile(b, i))),
                  pl.BlockSpec((tq, 128), lambda b, i: (tile(b, i), 0)),
                  pl.BlockSpec((SWA_BLOCK, 128), lambda b, i: (prev_blk(b, i), 0)),
                  pl.BlockSpec((128, tq), lambda b, i: (0, tile(b, i))),
                  pl.BlockSpec((128, SWA_BLOCK), lambda b, i: (0, prev_blk(b, i)))] + [c[0] for c in casts],
        out_specs=[pl.BlockSpec((SWA_QH * SWA_HD, tq), lambda b, i: (0, tile(b, i)))] + [c[1] for c in casts],
        out_shape=[jax.ShapeDtypeStruct((SWA_QH * SWA_HD, n), BF16)] + [c[2] for c in casts],
        compiler_params=_params("parallel", "arbitrary"),
        name="swa_core",
    )(sinks, qt, kp, kp, vt, vt, wu, wd)


MLP_FF_CHUNK = 512


def _dense_kernel(x_ref, o_ref, wo_ref, bo_ref, gpost_ref, g1_ref, g2_ref, wu_ref, wd_ref,
                  y_ref, a_ref, *, o_transposed):
    dims = TN_DIMS if o_transposed else (((1,), (0,)), ((), ()))
    tm = x_ref.shape[0]
    halves = [slice(0, tm // 2), slice(tm // 2, tm)]
    xs, hs = [], []
    for r in halves:
        o = o_ref[:, r] if o_transposed else o_ref[r, :]
        m = lax.dot_general(o, wo_ref[...], dims, preferred_element_type=F32) + bo_ref[...]
        x = x_ref[r, :] + _rms(m, gpost_ref[...])
        xs.append(x)
        hs.append(_rms(x, g1_ref[...]).astype(BF16))
    for r, h in zip(halves, hs):
        for j in range(D_FF // MLP_FF_CHUNK):
            cols = slice(j * MLP_FF_CHUNK, (j + 1) * MLP_FF_CHUNK)
            u = jnp.maximum(jnp.dot(h, wu_ref[:, cols], preferred_element_type=F32), 0.0)
            a_ref[r, cols] = (u * u).astype(BF16)
    for r, x in zip(halves, xs):
        f = jnp.dot(a_ref[r, :], wd_ref[...], preferred_element_type=F32)
        y_ref[r, :] = x + _rms(f, g2_ref[...])


def _dense(x, o, wo, bo, gpost, g1, g2, wu, wd, layer, j, tm, o_transposed):
    n = x.shape[0]
    tok = pl.BlockSpec((tm, D_MODEL), lambda i: (i, 0))
    o_spec = pl.BlockSpec((1024, tm), lambda i: (0, i)) if o_transposed else tok
    return pl.pallas_call(
        functools.partial(_dense_kernel, o_transposed=o_transposed),
        grid=(n // tm,),
        in_specs=[tok, o_spec, _layer_spec(wo, j), _layer_spec(bo, j), _layer_spec(gpost, layer),
                  _layer_spec(g1, layer), _layer_spec(g2, layer), _layer_spec(wu, 0), _layer_spec(wd, 0)],
        out_specs=tok,
        out_shape=jax.ShapeDtypeStruct((n, D_MODEL), F32),
        scratch_shapes=[pltpu.VMEM((tm, D_FF), BF16)],
        compiler_params=_params("parallel"),
        name="dense",
    )(x, o, wo, bo, gpost, g1, g2, wu, wd)


def _split_bf16(w):
    hi = w.astype(BF16)
    return hi, (w - hi.astype(F32)).astype(BF16)


def _gla_weights(w_in, w_gate_up, b_gate_up, g_norm):
    layers = w_in.shape[0]
    w_main = w_in[:, :, :3072].astype(BF16)
    w_lr_t = jnp.swapaxes(w_in[:, :, 3072:], 1, 2).astype(BF16)
    wg_hi, wg_lo = _split_bf16(w_gate_up)
    gn = jnp.tile(g_norm, (1, GLA_HEADS)).reshape(layers, 1, GLA_DV)
    return (w_main, w_lr_t, jnp.concatenate([wg_hi, wg_hi, wg_lo], axis=1),
            b_gate_up.reshape(layers, 1, GLA_DK), gn)


def _swa_weights(w_in, b_in):
    layers = w_in.shape[0]
    nq = SWA_QH * SWA_HD
    qscale = SWA_HD ** -0.5 * LOG2E
    wqt = jnp.swapaxes((w_in[:, :, :nq] * qscale).astype(BF16), 1, 2)
    bqt = (b_in[:, :nq] * qscale).reshape(layers, nq, 1)
    wk = w_in[:, :, nq:nq + 128].astype(BF16)
    bk = b_in[:, nq:nq + 128].reshape(layers, 1, 128)
    wvt = jnp.swapaxes(w_in[:, :, nq + 128:], 1, 2).astype(BF16)
    bvt = b_in[:, nq + 128:].reshape(layers, 128, 1)
    return wqt, bqt, wk, bk, wvt, bvt


def kernel(x, ln_mix_pre, ln_mix_post, ln_mlp_pre, ln_mlp_post, gla_w_in, gla_w_gate_up, gla_b_gate_up, gla_g_norm, gla_w_out, swa_w_in, swa_b_in, swa_sinks, swa_w_out, swa_b_out, mlp_w_up, mlp_w_down):
    batch, seq, d = x.shape
    n = batch * seq
    depth = ln_mix_pre.shape[0]
    tp, tm = min(PROJ_TILE, n), min(DENSE_TILE, n)
    tc, tq = min(GLA_CORE_TILE, seq), min(SWA_CORE_TILE, seq)
    xf = x.reshape(n, d)
    gain = lambda t: t.reshape(t.shape[0], 1, t.shape[1])
    g_pre, g_post, g1, g2 = gain(ln_mix_pre), gain(ln_mix_post), gain(ln_mlp_pre), gain(ln_mlp_post)
    gla_w = _gla_weights(gla_w_in, gla_w_gate_up, gla_b_gate_up, gla_g_norm)
    swa_w = _swa_weights(swa_w_in, swa_b_in)
    gla_wo, swa_wo = gla_w_out.astype(BF16), swa_w_out.astype(BF16)
    gla_bo = jnp.zeros((gla_w_out.shape[0], 1, D_MODEL), F32)
    swa_bo = gain(swa_b_out)
    for i in range(depth):
        j = i // 2
        if i % 2 == 0:
            qd, ki, ke, v, sg, dec = _gla_in(xf, g_pre, *gla_w, i, j, tp)
            o, wu, wd = _gla_core(qd, ki, ke, v, sg, dec, mlp_w_up, mlp_w_down, i, batch, seq, tc)
            wo, bo = gla_wo, gla_bo
        else:
            qt, kp, vt = _swa_in(xf, g_pre, *swa_w, i, j, tp)
            o, wu, wd = _swa_core(swa_sinks[j], qt, kp, vt, mlp_w_up, mlp_w_down, i, batch, seq, tq)
            wo, bo = swa_wo, swa_bo
        xf = _dense(xf, o, wo, bo, g_post, g1, g2, wu, wd, i, j, tm, o_transposed=(i % 2 == 1))
    return xf.reshape(batch, seq, d)
```

```python
import functools
import math

import jax
import jax.numpy as jnp
from jax import lax
from jax.experimental import pallas as pl
from jax.experimental.pallas import tpu as pltpu

D_MODEL = 1024
NORM_EPS = 1e-6

GLA_HEADS = 4
GLA_DK = 512
GLA_DV = 1024
GLA_HK = 128
GLA_HV = 256
GLA_RANK = 16
GLA_RANK_PAD = 128
GLA_NORMALIZER = 16.0
GLA_CHUNK = 64

SWA_HD = 64
SWA_QH = 16
SWA_GROUP = 8
SWA_KVH = 2
SWA_BLOCK = 128
SWA_QT_ROWS = SWA_QH * SWA_HD
SWA_ONES_ROWS = 16

D_FF = 4096

F32 = jnp.float32
BF16 = jnp.bfloat16

VMEM_LIMIT_BYTES = 56 * 1024 * 1024

PROJ_TILE = 1024
GLA_CORE_TILE = 2048
SWA_CORE_TILE = 1024
DENSE_TILE = 1024

LOG2E = math.log2(math.e)

NT_DIMS = (((1,), (1,)), ((), ()))
TN_DIMS = (((0,), (0,)), ((), ()))


def _rms(x, g):
    ms = jnp.mean(x * x, axis=-1, keepdims=True)
    return x * lax.rsqrt(ms + NORM_EPS) * g


def _layer_spec(stacked, layer):
    nd = stacked.ndim - 1
    return pl.BlockSpec((None,) + stacked.shape[1:], lambda *_: (layer,) + (0,) * nd,
                        pipeline_mode=pl.Buffered(1))


def _params(*sem):
    return pltpu.CompilerParams(dimension_semantics=sem, vmem_limit_bytes=VMEM_LIMIT_BYTES)


def _gla_proj_steps(x_ref, g_ref, w_ref, wlrt_ref, wg3_ref, bg_ref, gn_ref, dst, chunks):
    qd_ref, ki_ref, ke_ref, v_ref, sg_ref, dec_ref = dst
    c = GLA_CHUNK
    h = _rms(x_ref[...], g_ref[...]).astype(BF16)
    glr_t = lax.dot_general(wlrt_ref[...], h, NT_DIMS, preferred_element_type=F32)
    hi = glr_t.astype(BF16)
    lo = (glr_t - hi.astype(F32)).astype(BF16)
    z = lax.dot_general(jnp.concatenate([hi, lo, hi], axis=0), wg3_ref[...], TN_DIMS,
                        preferred_element_type=F32) + bg_ref[...]
    la = (jnp.minimum(z, 0.0) - jnp.log(1.0 + jnp.exp(-jnp.abs(z)))) * (1.0 / GLA_NORMALIZER)
    la_hi = la.astype(BF16)
    la_lo = (la - la_hi.astype(F32)).astype(BF16)

    row = lax.broadcasted_iota(jnp.int32, (c, 2 * c), 0)
    col = lax.broadcasted_iota(jnp.int32, (c, 2 * c), 1)
    tril2 = jnp.where((col % c) <= row, 1.0, 0.0).astype(BF16)

    yield
    half = GLA_DV // 2
    qf = jnp.dot(h, w_ref[:, :GLA_DK], preferred_element_type=F32)
    yield
    for p in range(2):
        gate = jnp.dot(h, w_ref[:, 2048 + p * half:2048 + (p + 1) * half], preferred_element_type=F32)
        sg_ref[:, p * half:(p + 1) * half] = (gate * jax.nn.sigmoid(gate)
                                              * gn_ref[:, p * half:(p + 1) * half]).astype(BF16)
        yield
    kf = jnp.dot(h, w_ref[:, GLA_DK:2 * GLA_DK], preferred_element_type=F32)
    yield
    scale = GLA_HK ** -0.5
    bcs = [jnp.dot(tril2, jnp.concatenate([la_hi[ci * c:(ci + 1) * c], la_lo[ci * c:(ci + 1) * c]], axis=0),
                   preferred_element_type=F32) for ci in range(chunks)]
    yield
    for p in range(2):
        v_ref[:, p * half:(p + 1) * half] = jnp.dot(h, w_ref[:, 1024 + p * half:1024 + (p + 1) * half],
                                                    preferred_element_type=F32).astype(BF16)
        for ci in range(p * chunks // 2, (p + 1) * chunks // 2):
            rows = slice(ci * c, (ci + 1) * c)
            bc = bcs[ci] * LOG2E
            b_last = bc[c - 1:c, :]
            k = kf[rows, :]
            qd_ref[rows, :] = (qf[rows, :] * (scale * jnp.exp2(bc))).astype(BF16)
            ki_ref[rows, :] = (k * jnp.exp2(-bc)).astype(BF16)
            ke_ref[rows, :] = (k * jnp.exp2(b_last - bc)).astype(BF16)
            dec_ref[ci:ci + 1, :] = jnp.exp2(b_last)
        yield


def _gla_core_steps(src, o_ref, st_ref, chunks):
    qd_ref, ki_ref, ke_ref, v_ref, sg_ref, dec_ref = src
    c = GLA_CHUNK
    row = lax.broadcasted_iota(jnp.int32, (c, c), 0)
    col = lax.broadcasted_iota(jnp.int32, (c, c), 1)
    causal = col <= row

    def scores(ci):
        rows = slice(ci * c, (ci + 1) * c)
        return [lax.dot_general(qd_ref[rows, h * GLA_HK:(h + 1) * GLA_HK],
                                ki_ref[rows, h * GLA_HK:(h + 1) * GLA_HK], NT_DIMS,
                                preferred_element_type=F32) for h in range(GLA_HEADS)]

    att_next = scores(0)
    for ci in range(chunks):
        rows = slice(ci * c, (ci + 1) * c)
        att_all = att_next
        if ci + 1 < chunks:
            att_next = scores(ci + 1)
        decay = dec_ref[ci:ci + 1, :]
        for h in range(GLA_HEADS):
            ks = slice(h * GLA_HK, (h + 1) * GLA_HK)
            vs = slice(h * GLA_HV, (h + 1) * GLA_HV)
            v = v_ref[rows, vs]
            att = jnp.where(causal, att_all[h], 0.0).astype(BF16)
            st = st_ref[h]
            o = (jnp.dot(att, v, preferred_element_type=F32)
                 + lax.dot_general(qd_ref[rows, ks], st.astype(BF16), NT_DIMS,
                                   preferred_element_type=F32))
            st_ref[h] = st * decay[:, ks] + lax.dot_general(v, ke_ref[rows, ks], TN_DIMS,
                                                            preferred_element_type=F32)
            ms = jnp.mean(o * o, axis=-1, keepdims=True)
            o_ref[rows, vs] = (o * lax.rsqrt(ms + NORM_EPS) * sg_ref[rows, vs].astype(F32)).astype(BF16)
        yield


def _gla_in_kernel(x_ref, g_ref, w_ref, wlrt_ref, wg3_ref, bg_ref, gn_ref, *dst, chunks):
    for _ in _gla_proj_steps(x_ref, g_ref, w_ref, wlrt_ref, wg3_ref, bg_ref, gn_ref, dst, chunks):
        pass


def _gla_in(x, g, w, wlrt, wg3, bg, gn, layer, j, tm):
    n = x.shape[0]
    chunks = tm // GLA_CHUNK
    tok = lambda c: pl.BlockSpec((tm, c), lambda i: (i, 0))
    return pl.pallas_call(
        functools.partial(_gla_in_kernel, chunks=chunks),
        grid=(n // tm,),
        in_specs=[tok(D_MODEL), _layer_spec(g, layer), _layer_spec(w, j), _layer_spec(wlrt, j),
                  _layer_spec(wg3, j), _layer_spec(bg, j), _layer_spec(gn, j)],
        out_specs=[tok(GLA_DK), tok(GLA_DK), tok(GLA_DK), tok(1024), tok(1024),
                   pl.BlockSpec((chunks, GLA_DK), lambda i: (i, 0))],
        out_shape=[jax.ShapeDtypeStruct((n, GLA_DK), BF16)] * 3 + [jax.ShapeDtypeStruct((n, 1024), BF16)] * 2
                  + [jax.ShapeDtypeStruct((n // GLA_CHUNK, GLA_DK), F32)],
        compiler_params=_params("parallel"),
        name="gla_in",
    )(x, g, w, wlrt, wg3, bg, gn)


def _cast_specs(stacked, layer, steps, step_of):
    rows, cols = stacked.shape[1:]
    slab = rows // steps
    return (pl.BlockSpec((None, slab, cols), lambda *g: (layer, step_of(*g), 0)),
            pl.BlockSpec((None, slab, cols), lambda *g: (0, step_of(*g), 0)),
            jax.ShapeDtypeStruct((1, rows, cols), BF16))


def _gla_core_kernel(qd_ref, ki_ref, ke_ref, v_ref, sg_ref, dec_ref, wu_ref, wd_ref,
                     o_ref, wub_ref, wdb_ref, st_ref, *, chunks):
    @pl.when(pl.program_id(1) == 0)
    def _():
        st_ref[...] = jnp.zeros_like(st_ref)

    wub_ref[...] = wu_ref[...].astype(BF16)
    wdb_ref[...] = wd_ref[...].astype(BF16)
    for _ in _gla_core_steps((qd_ref, ki_ref, ke_ref, v_ref, sg_ref, dec_ref), o_ref, st_ref, chunks):
        pass


def _gla_core(qd, ki, ke, v, sg, dec, wu, wd, layer, batch, seq, tc):
    n = batch * seq
    steps = seq // tc
    chunks = tc // GLA_CHUNK
    tile = lambda b, i: b * steps + i
    tok = lambda c: pl.BlockSpec((tc, c), lambda b, i: (tile(b, i), 0))
    casts = [_cast_specs(w, layer, batch * steps, tile) for w in (wu, wd)]
    return pl.pallas_call(
        functools.partial(_gla_core_kernel, chunks=chunks),
        grid=(batch, steps),
        in_specs=[tok(GLA_DK), tok(GLA_DK), tok(GLA_DK), tok(1024), tok(1024),
                  pl.BlockSpec((chunks, GLA_DK), lambda b, i: (tile(b, i), 0))] + [c[0] for c in casts],
        out_specs=[tok(1024)] + [c[1] for c in casts],
        out_shape=[jax.ShapeDtypeStruct((n, GLA_DV), BF16)] + [c[2] for c in casts],
        scratch_shapes=[pltpu.VMEM((GLA_HEADS, GLA_HV, GLA_HK), F32)],
        compiler_params=_params("parallel", "arbitrary"),
        name="gla_core",
    )(qd, ki, ke, v, sg, dec, wu, wd)


def _swa_in_kernel(x_ref, g_ref, wqt_ref, bqt_ref, wk_ref, bk_ref, wvt_ref, bvt_ref,
                   qt_ref, kp_ref, vt_ref):
    h = _rms(x_ref[...], g_ref[...]).astype(BF16)
    kp_ref[...] = (jnp.dot(h, wk_ref[...], preferred_element_type=F32) + bk_ref[...]).astype(BF16)
    vt_ref[...] = (lax.dot_general(wvt_ref[...], h, NT_DIMS, preferred_element_type=F32)
                   + bvt_ref[...]).astype(BF16)
    qt_ref[...] = (lax.dot_general(wqt_ref[...], h, NT_DIMS, preferred_element_type=F32)
                   + bqt_ref[...]).astype(BF16)


def _swa_in(x, g, wqt, bqt, wk, bk, wvt, bvt, layer, j, tm):
    n = x.shape[0]
    consts = (wqt, bqt, wk, bk, wvt, bvt)
    return pl.pallas_call(
        _swa_in_kernel,
        grid=(n // tm,),
        in_specs=[pl.BlockSpec((tm, D_MODEL), lambda i: (i, 0)), _layer_spec(g, layer)]
                 + [_layer_spec(c, j) for c in consts],
        out_specs=[pl.BlockSpec((SWA_QT_ROWS, tm), lambda i: (0, i)),
                   pl.BlockSpec((tm, 128), lambda i: (i, 0)),
                   pl.BlockSpec((128, tm), lambda i: (0, i))],
        out_shape=[jax.ShapeDtypeStruct((SWA_QT_ROWS, n), BF16),
                   jax.ShapeDtypeStruct((n, 128), BF16),
                   jax.ShapeDtypeStruct((128, n), BF16)],
        compiler_params=_params("parallel"),
        name="swa_in",
    )(x, g, *consts)


def _swa_core_kernel(sink_ref, qt_ref, kp_ref, kpp_ref, vt_ref, vtp_ref, wu_ref, wd_ref,
                     ot_ref, wub_ref, wdb_ref, *, blocks):
    wub_ref[...] = wu_ref[...].astype(BF16)
    wdb_ref[...] = wd_ref[...].astype(BF16)
    first_tile = pl.program_id(1) == 0
    blk = SWA_BLOCK
    sj = lax.broadcasted_iota(jnp.int32, (blk, blk), 0)
    qi = lax.broadcasted_iota(jnp.int32, (blk, blk), 1)
    prev_valid = sj > qi
    prev_mask = jnp.where(prev_valid, 1.0, 0.0).astype(BF16)
    cur_mask = jnp.where(prev_valid, 0.0, 1.0).astype(BF16)
    ones = jnp.ones((SWA_ONES_ROWS, 2 * blk), BF16)
    klane = lax.broadcasted_iota(jnp.int32, (2 * blk, SWA_KVH * SWA_HD), 1)

    def heads_of(kh):
        return [kh * SWA_GROUP + g for g in range(SWA_GROUP)]

    def scores(j, kh):
        cols = slice(j * blk, (j + 1) * blk)
        k_prev = kpp_ref[...] if j == 0 else kp_ref[(j - 1) * blk:j * blk, :]
        k_band = jnp.concatenate([k_prev, kp_ref[cols, :]], axis=0)
        k_band = jnp.where((klane // SWA_HD) == kh, k_band, jnp.zeros_like(k_band))
        first = lambda hq: (hq - kh) * SWA_HD
        qt = jnp.concatenate([qt_ref[first(hq):first(hq) + 2 * SWA_HD, cols] for hq in heads_of(kh)], axis=1)
        return jnp.dot(k_band, qt, preferred_element_type=F32)

    def finish(j, kh, s):
        cols = slice(j * blk, (j + 1) * blk)
        es, mxs = [], []
        for g, hq in enumerate(heads_of(kh)):
            sink = sink_ref[hq] * LOG2E
            s_prev = s[:blk, g * blk:(g + 1) * blk]
            if j == 0:
                s_prev = jnp.where(first_tile, -jnp.inf, s_prev)
            sm = jnp.where(prev_valid, s_prev, s[blk:, g * blk:(g + 1) * blk])
            mx = jnp.maximum(jnp.max(sm, axis=0, keepdims=True), sink)
            em = jnp.exp2(sm - mx).astype(BF16)
            es.append(jnp.concatenate([em * prev_mask, em * cur_mask], axis=0))
            mxs.append(jnp.exp2(sink - mx))
        e = jnp.concatenate(es, axis=1)
        rows = slice(kh * SWA_HD, (kh + 1) * SWA_HD)
        vt_prev = vtp_ref[rows, :] if j == 0 else vt_ref[rows, (j - 1) * blk:j * blk]
        lhs = jnp.concatenate([jnp.concatenate([vt_prev, vt_ref[rows, cols]], axis=1), ones], axis=0)
        o_aug = jnp.dot(lhs, e, preferred_element_type=F32)
        den = o_aug[SWA_HD:SWA_HD + 1, :] + jnp.concatenate(mxs, axis=1)
        o = (o_aug[:SWA_HD, :] / den).astype(BF16)
        for g, hq in enumerate(heads_of(kh)):
            ot_ref[hq * SWA_HD:(hq + 1) * SWA_HD, cols] = o[:, g * blk:(g + 1) * blk]

    units = [(j, kh) for j in range(blocks) for kh in range(SWA_KVH)]
    s_next = scores(*units[0])
    for idx, unit in enumerate(units):
        s = s_next
        if idx + 1 < len(units):
            s_next = scores(*units[idx + 1])
        finish(*unit, s)


def _swa_core(sinks, qt, kp, vt, wu, wd, layer, batch, seq, tq):
    n = batch * seq
    steps = seq // tq
    bpt = tq // SWA_BLOCK
    tile = lambda b, i: b * steps + i
    prev_blk = lambda b, i: jnp.maximum(tile(b, i) * bpt - 1, 0)
    casts = [_cast_specs(w, layer, batch * steps, tile) for w in (wu, wd)]
    return pl.pallas_call(
        functools.partial(_swa_core_kernel, blocks=bpt),
        grid=(batch, steps),
        in_specs=[pl.BlockSpec(memory_space=pltpu.SMEM),
                  pl.BlockSpec((SWA_QT_ROWS, tq), lambda b, i: (0, tile(b, i))),
                  pl.BlockSpec((tq, 128), lambda b, i: (tile(b, i), 0)),
                  pl.BlockSpec((SWA_BLOCK, 128), lambda b, i: (prev_blk(b, i), 0)),
                  pl.BlockSpec((128, tq), lambda b, i: (0, tile(b, i))),
                  pl.BlockSpec((128, SWA_BLOCK), lambda b, i: (0, prev_blk(b, i)))] + [c[0] for c in casts],
        out_specs=[pl.BlockSpec((SWA_QH * SWA_HD, tq), lambda b, i: (0, tile(b, i)))] + [c[1] for c in casts],
        out_shape=[jax.ShapeDtypeStruct((SWA_QH * SWA_HD, n), BF16)] + [c[2] for c in casts],
        compiler_params=_params("parallel", "arbitrary"),
        name="swa_core",
    )(sinks, qt, kp, kp, vt, vt, wu, wd)


MLP_FF_CHUNK = 512


def _dense_kernel(x_ref, o_ref, wo_ref, bo_ref, gpost_ref, g1_ref, g2_ref, wu_ref, wd_ref,
                  y_ref, a_ref, *, o_transposed):
    dims = TN_DIMS if o_transposed else (((1,), (0,)), ((), ()))
    tm = x_ref.shape[0]
    halves = [slice(0, tm // 2), slice(tm // 2, tm)]
    xs, hs = [], []
    for r in halves:
        o = o_ref[:, r] if o_transposed else o_ref[r, :]
        m = lax.dot_general(o, wo_ref[...], dims, preferred_element_type=F32) + bo_ref[...]
        x = x_ref[r, :] + _rms(m, gpost_ref[...])
        xs.append(x)
        hs.append(_rms(x, g1_ref[...]).astype(BF16))
    for r, h in zip(halves, hs):
        for j in range(D_FF // MLP_FF_CHUNK):
            cols = slice(j * MLP_FF_CHUNK, (j + 1) * MLP_FF_CHUNK)
            u = jnp.maximum(jnp.dot(h, wu_ref[:, cols], preferred_element_type=F32), 0.0)
            a_ref[r, cols] = (u * u).astype(BF16)
    for r, x in zip(halves, xs):
        f = jnp.dot(a_ref[r, :], wd_ref[...], preferred_element_type=F32)
        y_ref[r, :] = x + _rms(f, g2_ref[...])


def _dense(x, o, wo, bo, gpost, g1, g2, wu, wd, layer, j, tm, o_transposed):
    n = x.shape[0]
    tok = pl.BlockSpec((tm, D_MODEL), lambda i: (i, 0))
    o_spec = pl.BlockSpec((1024, tm), lambda i: (0, i)) if o_transposed else tok
    return pl.pallas_call(
        functools.partial(_dense_kernel, o_transposed=o_transposed),
        grid=(n // tm,),
        in_specs=[tok, o_spec, _layer_spec(wo, j), _layer_spec(bo, j), _layer_spec(gpost, layer),
                  _layer_spec(g1, layer), _layer_spec(g2, layer), _layer_spec(wu, 0), _layer_spec(wd, 0)],
        out_specs=tok,
        out_shape=jax.ShapeDtypeStruct((n, D_MODEL), F32),
        scratch_shapes=[pltpu.VMEM((tm, D_FF), BF16)],
        compiler_params=_params("parallel"),
        name="dense",
    )(x, o, wo, bo, gpost, g1, g2, wu, wd)


def _split_bf16(w):
    hi = w.astype(BF16)
    return hi, (w - hi.astype(F32)).astype(BF16)


def _gla_weights(w_in, w_gate_up, b_gate_up, g_norm):
    layers = w_in.shape[0]
    w_main = w_in[:, :, :3072].astype(BF16)
    w_lr_t = jnp.swapaxes(w_in[:, :, 3072:], 1, 2).astype(BF16)
    wg_hi, wg_lo = _split_bf16(w_gate_up)
    gn = jnp.tile(g_norm, (1, GLA_HEADS)).reshape(layers, 1, GLA_DV)
    return (w_main, w_lr_t, jnp.concatenate([wg_hi, wg_hi, wg_lo], axis=1),
            b_gate_up.reshape(layers, 1, GLA_DK), gn)


def _swa_weights(w_in, b_in):
    layers = w_in.shape[0]
    nq = SWA_QH * SWA_HD
    qscale = SWA_HD ** -0.5 * LOG2E
    wqt = jnp.swapaxes((w_in[:, :, :nq] * qscale).astype(BF16), 1, 2)
    bqt = (b_in[:, :nq] * qscale).reshape(layers, nq, 1)
    wk = w_in[:, :, nq:nq + 128].astype(BF16)
    bk = b_in[:, nq:nq + 128].reshape(layers, 1, 128)
    wvt = jnp.swapaxes(w_in[:, :, nq + 128:], 1, 2).astype(BF16)
    bvt = b_in[:, nq + 128:].reshape(layers, 128, 1)
    return wqt, bqt, wk, bk, wvt, bvt


def kernel(x, ln_mix_pre, ln_mix_post, ln_mlp_pre, ln_mlp_post, gla_w_in, gla_w_gate_up, gla_b_gate_up, gla_g_norm, gla_w_out, swa_w_in, swa_b_in, swa_sinks, swa_w_out, swa_b_out, mlp_w_up, mlp_w_down):
    batch, seq, d = x.shape
    n = batch * seq
    depth = ln_mix_pre.shape[0]
    tp, tm = min(PROJ_TILE, n), min(DENSE_TILE, n)
    tc, tq = min(GLA_CORE_TILE, seq), min(SWA_CORE_TILE, seq)
    xf = x.reshape(n, d)
    gain = lambda t: t.reshape(t.shape[0], 1, t.shape[1])
    g_pre, g_post, g1, g2 = gain(ln_mix_pre), gain(ln_mix_post), gain(ln_mlp_pre), gain(ln_mlp_post)
    gla_w = _gla_weights(gla_w_in, gla_w_gate_up, gla_b_gate_up, gla_g_norm)
    swa_w = _swa_weights(swa_w_in, swa_b_in)
    gla_wo, swa_wo = gla_w_out.astype(BF16), swa_w_out.astype(BF16)
    gla_bo = jnp.zeros((gla_w_out.shape[0], 1, D_MODEL), F32)
    swa_bo = gain(swa_b_out)
    for i in range(depth):
        j = i // 2
        if i % 2 == 0:
            qd, ki, ke, v, sg, dec = _gla_in(xf, g_pre, *gla_w, i, j, tp)
            o, wu, wd = _gla_core(qd, ki, ke, v, sg, dec, mlp_w_up, mlp_w_down, i, batch, seq, tc)
            wo, bo = gla_wo, gla_bo
        else:
            qt, kp, vt = _swa_in(xf, g_pre, *swa_w, i, j, tp)
            o, wu, wd = _swa_core(swa_sinks[j], qt, kp, vt, mlp_w_up, mlp_w_down, i, batch, seq, tq)
            wo, bo = swa_wo, swa_bo
        xf = _dense(xf, o, wo, bo, g_post, g1, g2, wu, wd, i, j, tm, o_transposed=(i % 2 == 1))
    return xf.reshape(batch, seq, d)
```
